```python
import math
import jax, jax.numpy as jnp
from jax import lax
import numpy as np

D_MODEL = 1024
BATCH = 2
SEQ = 16384
DEPTH = 2
DEC_BATCH = 8
DEC_SEQ = 4096
PAST_LEN = 128

GRID_W = 64
N_MIXERS = 2
N_A_LAYERS = (DEPTH + 1) // 2
N_B_LAYERS = DEPTH // 2
NA_HEADS = 16
NA_HEAD_DIM = D_MODEL // NA_HEADS
NA_WIN_H = 8
NA_WIN_W = 16
MLA_HEADS = 16
MLA_Q_RANK = 384
MLA_KV_RANK = 256
MLA_NOPE = 64
MLA_ROPE = 32
MLA_V = 64
MLA_QK = MLA_NOPE + MLA_ROPE
ROPE_THETA = 10000.0
Q_BLOCK = 128
N_EXPERTS = 64
TOP_K = 6
N_GROUPS = 8
TOPK_GROUPS = 4
EXPERT_FF = 256
SHARED_FF = 256
ROUTED_SCALE = 2.5
TOKEN_BLOCK = 128
EPS = 1e-6

kernel_name = "hybrid_natten_mla_moe_adaln_encoder"


def rms_norm(x, g):
    xf = x.astype(jnp.float32)
    y = xf * lax.rsqrt(jnp.mean(xf * xf, axis=-1, keepdims=True) + EPS)
    return (y * g.astype(jnp.float32)).astype(x.dtype)


def neighbourhood_attention(h, w_qkv, g_q, g_k, rpb, w_o):
    B, S, D = h.shape
    rows = S // GRID_W
    kh = min(NA_WIN_H, rows)
    qkv = (h @ w_qkv).reshape(B, rows, GRID_W, 3, NA_HEADS, NA_HEAD_DIM)
    q = rms_norm(qkv[:, :, :, 0], g_q)
    k = rms_norm(qkv[:, :, :, 1], g_k)
    v = qkv[:, :, :, 2]
    scale = NA_HEAD_DIM ** -0.5
    cols = jnp.arange(GRID_W)
    col_start = jnp.clip(cols - NA_WIN_W // 2, 0, GRID_W - NA_WIN_W)
    col_idx = col_start[:, None] + jnp.arange(NA_WIN_W)[None, :]
    dc = col_idx - cols[:, None]

    def row_block(r):
        row_start = jnp.clip(r - kh // 2, 0, rows - kh)
        k_rows = lax.dynamic_slice_in_dim(k, row_start, kh, axis=1)
        v_rows = lax.dynamic_slice_in_dim(v, row_start, kh, axis=1)
        k_nb = k_rows[:, :, col_idx]
        v_nb = v_rows[:, :, col_idx]
        q_r = lax.dynamic_index_in_dim(q, r, axis=1, keepdims=False)
        s = jnp.einsum('bqhd,bjqkhd->bhqjk', q_r, k_nb).astype(jnp.float32) * scale
        dr = row_start + jnp.arange(kh) - r
        bias = rpb[:, dr[None, :, None] + NA_WIN_H - 1, dc[:, None, :] + NA_WIN_W - 1]
        s = s + bias[None].astype(jnp.float32)
        p = jax.nn.softmax(s.reshape(B, NA_HEADS, GRID_W, kh * NA_WIN_W), axis=-1)
        p = p.reshape(B, NA_HEADS, GRID_W, kh, NA_WIN_W).astype(v.dtype)
        return jnp.einsum('bhqjk,bjqkhd->bqhd', p, v_nb)

    o = lax.map(row_block, jnp.arange(rows))
    o = o.transpose(1, 0, 2, 3, 4).reshape(B, S, D)
    return o @ w_o


def rope_tables(S):
    half = MLA_ROPE // 2
    inv = 1.0 / (ROPE_THETA ** (jnp.arange(half, dtype=jnp.float32) / half))
    ang = jnp.arange(S, dtype=jnp.float32)[:, None] * inv[None, :]
    return jnp.cos(ang), jnp.sin(ang)


def apply_rope(x, cos, sin):
    half = MLA_ROPE // 2
    xf = x.astype(jnp.float32)
    x1, x2 = xf[..., :half], xf[..., half:]
    c = cos[:, None, :]
    s = sin[:, None, :]
    return jnp.concatenate([x1 * c - x2 * s, x1 * s + x2 * c], axis=-1).astype(x.dtype)


def latent_attention(h, w_down, g_cq, g_ckv, w_uq, w_ukv, g_q, g_k, w_o):
    B, S, D = h.shape
    down = h @ w_down
    c_q = rms_norm(down[..., :MLA_Q_RANK], g_cq)
    c_kv = rms_norm(down[..., MLA_Q_RANK:MLA_Q_RANK + MLA_KV_RANK], g_ckv)
    k_pe = down[..., MLA_Q_RANK + MLA_KV_RANK:][:, :, None, :]
    q = (c_q @ w_uq).reshape(B, S, MLA_HEADS, MLA_QK)
    kv = (c_kv @ w_ukv).reshape(B, S, MLA_HEADS, MLA_NOPE + MLA_V)
    k_nope, v = kv[..., :MLA_NOPE], kv[..., MLA_NOPE:]
    cos, sin = rope_tables(S)
    q_nope = rms_norm(q[..., :MLA_NOPE], g_q[:MLA_NOPE])
    q_pe = apply_rope(rms_norm(q[..., MLA_NOPE:], g_q[MLA_NOPE:]), cos, sin)
    k_nope = rms_norm(k_nope, g_k[:MLA_NOPE])
    k_pe = apply_rope(rms_norm(k_pe, g_k[MLA_NOPE:]), cos, sin)
    q = jnp.concatenate([q_nope, q_pe], axis=-1)
    k = jnp.concatenate([k_nope, jnp.broadcast_to(k_pe, (B, S, MLA_HEADS, MLA_ROPE))], axis=-1)
    scale = MLA_QK ** -0.5
    n_blocks = S // Q_BLOCK
    qb = q.reshape(B, n_blocks, Q_BLOCK, MLA_HEADS, MLA_QK).transpose(1, 0, 2, 3, 4)

    def attend(q_blk):
        s = jnp.einsum('bqhd,bkhd->bhqk', q_blk, k).astype(jnp.float32) * scale
        p = jax.nn.softmax(s, axis=-1).astype(v.dtype)
        return jnp.einsum('bhqk,bkhd->bqhd', p, v)

    o = lax.map(attend, qb)
    o = o.transpose(1, 0, 2, 3, 4).reshape(B, S, MLA_HEADS * MLA_V)
    return o @ w_o


def moe(h, w_router, b_router, w_gate, w_up, w_down, ws_gate, ws_up, ws_down):
    B, S, D = h.shape
    t = h.reshape(-1, D)
    N = t.shape[0]
    scores = jax.nn.sigmoid((t @ w_router).astype(jnp.float32))
    choice = scores + b_router.astype(jnp.float32)
    per_group = N_EXPERTS // N_GROUPS
    grp_score = lax.top_k(choice.reshape(N, N_GROUPS, per_group), 2)[0].sum(-1)
    _, top_groups = lax.top_k(grp_score, TOPK_GROUPS)
    group_mask = jax.nn.one_hot(top_groups, N_GROUPS, dtype=jnp.float32).sum(1) > 0
    expert_mask = jnp.repeat(group_mask, per_group, axis=1)
    masked = jnp.where(expert_mask, choice, -jnp.inf)
    _, top_idx = lax.top_k(masked, TOP_K)
    top_w = jnp.take_along_axis(scores, top_idx, axis=-1)
    top_w = top_w / jnp.sum(top_w, axis=-1, keepdims=True) * ROUTED_SCALE
    gates = jnp.sum(jax.nn.one_hot(top_idx, N_EXPERTS, dtype=jnp.float32) * top_w[..., None], axis=1)
    n_blocks = N // TOKEN_BLOCK

    def expert_block(args):
        xb, gb = args
        a = jnp.einsum('td,edf->tef', xb, w_gate)
        u = jnp.einsum('td,edf->tef', xb, w_up)
        hh = jax.nn.silu(a) * u * gb[..., None].astype(xb.dtype)
        return jnp.einsum('tef,efd->td', hh, w_down)

    routed = lax.map(expert_block, (t.reshape(n_blocks, TOKEN_BLOCK, D),
                                    gates.reshape(n_blocks, TOKEN_BLOCK, N_EXPERTS)))
    shared = (jax.nn.silu(t @ ws_gate) * (t @ ws_up)) @ ws_down
    return (routed.reshape(N, D) + shared).reshape(B, S, D)


def trunk(x, c, g_norm1, g_norm2, w_ada, b_ada,
          na_w_qkv, na_g_q, na_g_k, na_rpb, na_w_o,
          mla_w_down, mla_g_cq, mla_g_ckv, mla_w_uq, mla_w_ukv, mla_g_q, mla_g_k, mla_w_o,
          w_router, b_router, w_gate, w_up, w_down, ws_gate, ws_up, ws_down):
    B = x.shape[0]
    for i in range(DEPTH):
        mods = (jax.nn.silu(c) @ w_ada[i] + b_ada[i]).reshape(B, 6, D_MODEL)
        sh1, sc1, g1, sh2, sc2, g2 = [mods[:, j, None, :] for j in range(6)]
        hn = rms_norm(x, g_norm1[i]) * (1 + sc1) + sh1
        j = i // N_MIXERS
        if i % N_MIXERS == 0:
            mix = neighbourhood_attention(hn, na_w_qkv[j], na_g_q[j], na_g_k[j], na_rpb[j], na_w_o[j])
        else:
            mix = latent_attention(hn, mla_w_down[j], mla_g_cq[j], mla_g_ckv[j], mla_w_uq[j],
                                   mla_w_ukv[j], mla_g_q[j], mla_g_k[j], mla_w_o[j])
        x = x + g1 * mix
        hn = rms_norm(x, g_norm2[i]) * (1 + sc2) + sh2
        x = x + g2 * moe(hn, w_router[i], b_router[i], w_gate[i], w_up[i], w_down[i],
                         ws_gate[i], ws_up[i], ws_down[i])
    return x


def setup_inputs(seed: int = 0) -> dict:
    key = jax.random.key(seed)
    ks = iter(jax.random.split(key, 40))
    f32 = jnp.float32

    def nrm(shape, scale):
        return jax.random.normal(next(ks), shape, f32) * scale

    def gain(shape):
        return 1.0 + 0.02 * jax.random.normal(next(ks), shape, f32)

    D = D_MODEL
    E = N_EXPERTS
    return {
        "x_prompt": nrm((BATCH, SEQ, D), 1.0),
        "x_sample": nrm((DEC_BATCH, DEC_SEQ, D), 1.0),
        "c_prompt": nrm((BATCH, D), 1.0),
        "c_sample": nrm((DEC_BATCH, D), 1.0),
        "g_norm1": gain((DEPTH, D)),
        "g_norm2": gain((DEPTH, D)),
        "w_ada": nrm((DEPTH, D, 6 * D), 0.5 * D ** -0.5),
        "b_ada": nrm((DEPTH, 6 * D), 0.02),
        "na_w_qkv": nrm((N_A_LAYERS, D, 3 * D), D ** -0.5),
        "na_g_q": gain((N_A_LAYERS, NA_HEAD_DIM)),
        "na_g_k": gain((N_A_LAYERS, NA_HEAD_DIM)),
        "na_rpb": nrm((N_A_LAYERS, NA_HEADS, 2 * NA_WIN_H - 1, 2 * NA_WIN_W - 1), 0.02),
        "na_w_o": nrm((N_A_LAYERS, D, D), D ** -0.5),
        "mla_w_down": nrm((N_B_LAYERS, D, MLA_Q_RANK + MLA_KV_RANK + MLA_ROPE), D ** -0.5),
        "mla_g_cq": gain((N_B_LAYERS, MLA_Q_RANK)),
        "mla_g_ckv": gain((N_B_LAYERS, MLA_KV_RANK)),
        "mla_w_uq": nrm((N_B_LAYERS, MLA_Q_RANK, MLA_HEADS * MLA_QK), MLA_Q_RANK ** -0.5),
        "mla_w_ukv": nrm((N_B_LAYERS, MLA_KV_RANK, MLA_HEADS * (MLA_NOPE + MLA_V)), MLA_KV_RANK ** -0.5),
        "mla_g_q": gain((N_B_LAYERS, MLA_QK)),
        "mla_g_k": gain((N_B_LAYERS, MLA_QK)),
        "mla_w_o": nrm((N_B_LAYERS, MLA_HEADS * MLA_V, D), (MLA_HEADS * MLA_V) ** -0.5),
        "w_router": nrm((DEPTH, D, E), D ** -0.5),
        "b_router": nrm((DEPTH, E), 0.01),
        "w_gate": nrm((DEPTH, E, D, EXPERT_FF), D ** -0.5),
        "w_up": nrm((DEPTH, E, D, EXPERT_FF), D ** -0.5),
        "w_down": nrm((DEPTH, E, EXPERT_FF, D), EXPERT_FF ** -0.5),
        "ws_gate": nrm((DEPTH, D, SHARED_FF), D ** -0.5),
        "ws_up": nrm((DEPTH, D, SHARED_FF), D ** -0.5),
        "ws_down": nrm((DEPTH, SHARED_FF, D), SHARED_FF ** -0.5),
    }


def reference(x_prompt, x_sample, c_prompt, c_sample, g_norm1, g_norm2, w_ada, b_ada,
              na_w_qkv, na_g_q, na_g_k, na_rpb, na_w_o,
              mla_w_down, mla_g_cq, mla_g_ckv, mla_w_uq, mla_w_ukv, mla_g_q, mla_g_k, mla_w_o,
              w_router, b_router, w_gate, w_up, w_down, ws_gate, ws_up, ws_down):
    y_prompt = trunk(x_prompt, c_prompt, g_norm1, g_norm2, w_ada, b_ada,
                     na_w_qkv, na_g_q, na_g_k, na_rpb, na_w_o,
                     mla_w_down, mla_g_cq, mla_g_ckv, mla_w_uq, mla_w_ukv, mla_g_q, mla_g_k, mla_w_o,
                     w_router, b_router, w_gate, w_up, w_down, ws_gate, ws_up, ws_down)
    y_sample = trunk(x_sample, c_sample, g_norm1, g_norm2, w_ada, b_ada,
                     na_w_qkv, na_g_q, na_g_k, na_rpb, na_w_o,
                     mla_w_down, mla_g_cq, mla_g_ckv, mla_w_uq, mla_w_ukv, mla_g_q, mla_g_k, mla_w_o,
                     w_router, b_router, w_gate, w_up, w_down, ws_gate, ws_up, ws_down)
    return (y_prompt, y_sample)
```

```python
import functools
import math

import jax
import jax.numpy as jnp
import numpy as np
from jax import lax
from jax.experimental import pallas as pl
from jax.experimental.pallas import tpu as pltpu

F32 = jnp.float32
BF16 = jnp.bfloat16

D_MODEL = 1024
GRID_W = 64
NA_HEADS = 16
NA_HEAD_DIM = 64
NA_WIN_H = 8
NA_WIN_W = 16
NA_Q_ROWS = 4
NA_K_ROWS = 3 * NA_Q_ROWS
MLA_HEADS = 16
MLA_Q_RANK = 384
MLA_KV_RANK = 256
MLA_NOPE = 64
MLA_ROPE = 32
MLA_V = 64
MLA_QK = MLA_NOPE + MLA_ROPE
MLA_SLOT = 128
ROPE_THETA = 10000.0
N_EXPERTS = 64
TOP_K = 6
N_GROUPS = 8
TOPK_GROUPS = 4
EXPERT_FF = 256
ROUTED_SCALE = 2.5
EPS = 1e-6
NEG = -1e30

VMEM_LIMIT = 56 * 1024 * 1024
NT_DIMS = (((1,), (1,)), ((), ()))


def _params(*sem):
    return pltpu.CompilerParams(dimension_semantics=sem, vmem_limit_bytes=VMEM_LIMIT)


def _sigmoid(x):
    return 1.0 / (1.0 + jnp.exp(-x))


def _modnorm(x, g, sc, sh):
    ms = jnp.mean(x * x, axis=-1, keepdims=True)
    return (x * lax.rsqrt(ms + EPS) * g) * (1.0 + sc) + sh


def _rmsnorm_rows(x, g):
    ms = jnp.mean(x * x, axis=-1, keepdims=True)
    return x * lax.rsqrt(ms + EPS) * g


def _segnorm(y, p, g):
    ms = jnp.dot((y * y).astype(BF16), p, preferred_element_type=F32)
    return y * lax.rsqrt(ms + EPS) * g


def _mods_kernel(c_ref, w_ref, b_ref, o_ref):
    c = c_ref[...]
    a = c * _sigmoid(c)
    o_ref[...] = jnp.dot(a, w_ref[...], preferred_element_type=F32,
                         precision=lax.Precision.HIGHEST) + b_ref[...]


def _mods(c_all, w, b):
    rows = c_all.shape[0]
    n = w.shape[1]
    tn = 1024
    return pl.pallas_call(
        _mods_kernel,
        out_shape=jax.ShapeDtypeStruct((rows, n), F32),
        grid=(n // tn,),
        in_specs=[pl.BlockSpec((rows, D_MODEL), lambda j: (0, 0)),
                  pl.BlockSpec((D_MODEL, tn), lambda j: (0, j)),
                  pl.BlockSpec((1, tn), lambda j: (0, j))],
        out_specs=pl.BlockSpec((rows, tn), lambda j: (0, j)),
        compiler_params=_params("arbitrary"), name="adaln_mods",
    )(c_all, w, b.reshape(1, n))


def _na_qkv_kernel(x_ref, mods_ref, g_ref, w_ref, p_ref, gq_ref, gk_ref, q_ref, k_ref, v_ref):
    hn = _modnorm(x_ref[...], g_ref[...], mods_ref[1:2, :], mods_ref[0:1, :]).astype(BF16)
    for part, o_ref, gg in ((0, q_ref, gq_ref), (1, k_ref, gk_ref), (2, v_ref, None)):
        for t in range(D_MODEL // 256):
            lo = t * 256
            y = jnp.dot(hn, w_ref[:, part * D_MODEL + lo:part * D_MODEL + lo + 256],
                        preferred_element_type=F32)
            if gg is not None:
                y = _segnorm(y, p_ref[...], gg[:, lo:lo + 256])
            o_ref[:, lo:lo + 256] = y.astype(BF16)


def _na_qkv(x, mods, g, w, p, gq, gk, tm=512):
    B, S, _ = x.shape
    tok = pl.BlockSpec((None, tm, D_MODEL), lambda b, i: (b, i, 0))
    full = lambda a: pl.BlockSpec(a.shape, lambda b, i: (0,) * a.ndim)
    out = jax.ShapeDtypeStruct((B, S, D_MODEL), BF16)
    return pl.pallas_call(
        _na_qkv_kernel,
        out_shape=(out, out, out),
        grid=(B, S // tm),
        in_specs=[tok, pl.BlockSpec((None, 8, D_MODEL), lambda b, i: (b, 0, 0)),
                  full(g), full(w), full(p), full(gq), full(gk)],
        out_specs=(tok, tok, tok),
        compiler_params=_params("parallel", "parallel"), name="na_qkv",
    )(x, mods, g, w, p, gq, gk)


def _na_attn_kernel(q_ref, k0_ref, k1_ref, k2_ref, v0_ref, v1_ref, v2_ref, bias_ref, o_ref):
    lane = lax.broadcasted_iota(jnp.int32, (1, 128), 1)
    first = lane < NA_HEAD_DIM
    for hp in range(NA_HEADS // 2):
        cols = slice(hp * 128, (hp + 1) * 128)
        q = q_ref[:, cols]
        k = jnp.concatenate([k0_ref[:, cols], k1_ref[:, cols], k2_ref[:, cols]], axis=0)
        v = jnp.concatenate([v0_ref[:, cols], v1_ref[:, cols], v2_ref[:, cols]], axis=0)
        outs = []
        for sub in range(2):
            keep = first if sub == 0 else jnp.logical_not(first)
            qm = jnp.where(keep, q, jnp.zeros_like(q))
            s = lax.dot_general(qm, k, NT_DIMS, preferred_element_type=F32)
            s = s + bias_ref[2 * hp + sub].astype(F32)
            m = jnp.max(s, axis=-1, keepdims=True)
            e = jnp.exp(s - m)
            l = jnp.sum(e, axis=-1, keepdims=True)
            o = jnp.dot(e.astype(BF16), v, preferred_element_type=F32)
            outs.append(o / l)
        o_ref[:, cols] = jnp.where(first, outs[0], outs[1]).astype(BF16)


def _na_attn(q, k, v, bias):
    B, S, _ = q.shape
    tq = NA_Q_ROWS * GRID_W
    ng = S // tq
    assert ng >= 3

    def kv_spec(j):
        return pl.BlockSpec((None, tq, D_MODEL),
                            lambda b, g: (b, jnp.clip(g - 1, 0, ng - 3) + j, 0))

    def bias_map(b, g):
        return (0, jnp.where(g == 0, 0, jnp.where(g == ng - 1, 2, 1)), 0)

    qspec = pl.BlockSpec((None, tq, D_MODEL), lambda b, g: (b, g, 0))
    return pl.pallas_call(
        _na_attn_kernel,
        out_shape=jax.ShapeDtypeStruct((B, S, D_MODEL), BF16),
        grid=(B, ng),
        in_specs=[qspec, kv_spec(0), kv_spec(1), kv_spec(2), kv_spec(0), kv_spec(1), kv_spec(2),
                  pl.BlockSpec((NA_HEADS, tq, 3 * tq), bias_map)],
        out_specs=qspec,
        compiler_params=_params("parallel", "arbitrary"), name="na_attn",
    )(q, k, k, k, v, v, v, bias)


def _na_bias_table(rpb):
    rows = NA_K_ROWS
    r = np.arange(rows)
    c = np.arange(GRID_W)
    rs = np.clip(r - NA_WIN_H // 2, 0, rows - NA_WIN_H)
    cs = np.clip(c - NA_WIN_W // 2, 0, GRID_W - NA_WIN_W)
    dr = r[None, :] - r[:, None]
    dc = c[None, :] - c[:, None]
    ok_r = (r[None, :] >= rs[:, None]) & (r[None, :] < rs[:, None] + NA_WIN_H)
    ok_c = (c[None, :] >= cs[:, None]) & (c[None, :] < cs[:, None] + NA_WIN_W)
    ri = np.clip(dr + NA_WIN_H - 1, 0, 2 * NA_WIN_H - 2)
    ci = np.clip(dc + NA_WIN_W - 1, 0, 2 * NA_WIN_W - 2)
    ok = ok_r[:, None, :, None] & ok_c[None, :, None, :]
    tab = rpb[:, ri[:, None, :, None], ci[None, :, None, :]]
    tab = jnp.where(ok[None], tab, NEG)
    return tab.reshape(NA_HEADS, rows * GRID_W, rows * GRID_W).astype(BF16)


def _proj_res_kernel(o_ref, w_ref, x_ref, mods_ref, out_ref, *, gate_row):
    y = jnp.dot(o_ref[...], w_ref[...], preferred_element_type=F32)
    out_ref[...] = x_ref[...] + mods_ref[gate_row:gate_row + 1, :] * y


def _proj_res(o, w, x, mods, gate_row, tm=512):
    B, S, K = o.shape
    tok = pl.BlockSpec((None, tm, D_MODEL), lambda b, i: (b, i, 0))
    return pl.pallas_call(
        functools.partial(_proj_res_kernel, gate_row=gate_row),
        out_shape=jax.ShapeDtypeStruct((B, S, D_MODEL), F32),
        grid=(B, S // tm),
        in_specs=[pl.BlockSpec((None, tm, K), lambda b, i: (b, i, 0)),
                  pl.BlockSpec(w.shape, lambda b, i: (0, 0)),
                  tok, pl.BlockSpec((None, 8, D_MODEL), lambda b, i: (b, 0, 0))],
        out_specs=tok,
        compiler_params=_params("parallel", "parallel"), name="proj_res",
    )(o, w, x, mods)


def _rope(y, cosf, sinf, width):
    lane = lax.broadcasted_iota(jnp.int32, (1, width), 1) % MLA_SLOT
    half = MLA_ROPE // 2
    up = pltpu.roll(y, width - half, 1)
    dn = pltpu.roll(y, half, 1)
    lo = (lane >= MLA_NOPE) & (lane < MLA_NOPE + half)
    hi = (lane >= MLA_NOPE + half) & (lane < MLA_QK)
    rot = jnp.where(lo, -up, jnp.where(hi, dn, 0.0))
    return y * cosf + rot * sinf


def _mla_proj_kernel(x_ref, mods_ref, g_ref, wd_ref, gcq_ref, gckv_ref, wuq_ref, wukv_ref,
                     pq_ref, pk_ref, gq_ref, gkn_ref, gkpe_ref, cos_ref, sin_ref,
                     q_ref, k_ref, v_ref):
    hn = _modnorm(x_ref[...], g_ref[...], mods_ref[1:2, :], mods_ref[0:1, :]).astype(BF16)
    down = jnp.dot(hn, wd_ref[...], preferred_element_type=F32)
    cq = _rmsnorm_rows(down[:, :MLA_Q_RANK], gcq_ref[...]).astype(BF16)
    ckv = _rmsnorm_rows(down[:, MLA_Q_RANK:MLA_Q_RANK + MLA_KV_RANK], gckv_ref[...]).astype(BF16)
    cosf = cos_ref[...]
    sinf = sin_ref[...]
    kpe = down[:, MLA_Q_RANK + MLA_KV_RANK:]
    kpe = _segnorm(kpe, pq_ref[:MLA_SLOT, :MLA_SLOT], gkpe_ref[...])
    kpe = _rope(kpe, cosf[:, :MLA_SLOT], sinf[:, :MLA_SLOT], MLA_SLOT)
    kpe2 = jnp.concatenate([kpe, kpe], axis=1)
    lane = lax.broadcasted_iota(jnp.int32, (1, 256), 1) % MLA_SLOT
    ones_lane = jnp.where(lane == MLA_V, 1.0, 0.0)
    n_tiles = MLA_HEADS * MLA_SLOT // 256
    for t in range(n_tiles):
        cols = slice(t * 256, (t + 1) * 256)
        y = jnp.dot(cq, wuq_ref[:, cols], preferred_element_type=F32)
        y = _segnorm(y, pq_ref[...], gq_ref[...])
        q_ref[:, cols] = _rope(y, cosf, sinf, 256).astype(BF16)
    for t in range(n_tiles):
        cols = slice(t * 256, (t + 1) * 256)
        y = jnp.dot(ckv, wukv_ref[:, cols], preferred_element_type=F32)
        y = _segnorm(y, pk_ref[...], gkn_ref[...])
        k_ref[:, cols] = (y + kpe2).astype(BF16)
    off = MLA_HEADS * MLA_SLOT
    for t in range(n_tiles):
        cols = slice(t * 256, (t + 1) * 256)
        y = jnp.dot(ckv, wukv_ref[:, off + t * 256:off + (t + 1) * 256], preferred_element_type=F32)
        v_ref[:, cols] = (y + ones_lane).astype(BF16)


def _mla_proj(x, mods, g, wd, gcq, gckv, wuq, wukv, pq, pk, gq, gkn, gkpe, cosf, sinf, tm=512):
    B, S, _ = x.shape
    wide = MLA_HEADS * MLA_SLOT
    tok = pl.BlockSpec((None, tm, D_MODEL), lambda b, i: (b, i, 0))
    full = lambda a: pl.BlockSpec(a.shape, lambda b, i: (0,) * a.ndim)
    pos = pl.BlockSpec((tm, 256), lambda b, i: (i, 0))
    out = jax.ShapeDtypeStruct((B, S, wide), BF16)
    ospec = pl.BlockSpec((None, tm, wide), lambda b, i: (b, i, 0))
    return pl.pallas_call(
        _mla_proj_kernel,
        out_shape=(out, out, out),
        grid=(B, S // tm),
        in_specs=[tok, pl.BlockSpec((None, 8, D_MODEL), lambda b, i: (b, 0, 0)),
                  full(g), full(wd), full(gcq), full(gckv), full(wuq), full(wukv),
                  full(pq), full(pk), full(gq), full(gkn), full(gkpe), pos, pos],
        out_specs=(ospec, ospec, ospec),
        compiler_params=_params("parallel", "parallel"), name="mla_proj",
    )(x, mods, g, wd, gcq, gckv, wuq, wukv, pq, pk, gq, gkn, gkpe, cosf, sinf)


def _flash_kernel(q_ref, k_ref, v_ref, o_ref, *, tk):
    q = q_ref[...]
    tq = q.shape[0]
    nk = k_ref.shape[0] // tk

    def body(j, carry):
        m, acc = carry
        ks = pl.multiple_of(j * tk, tk)
        k = k_ref[pl.ds(ks, tk), :]
        v = v_ref[pl.ds(ks, tk), :]
        s = lax.dot_general(q, k, NT_DIMS, preferred_element_type=F32)
        m_new = jnp.maximum(m, jnp.max(s, axis=-1, keepdims=True))
        alpha = jnp.exp(m - m_new)
        p = jnp.exp(s - m_new)
        acc = alpha * acc + jnp.dot(p.astype(BF16), v, preferred_element_type=F32)
        return m_new, acc

    m0 = jnp.full((tq, 1), -jnp.inf, F32)
    acc0 = jnp.zeros((tq, MLA_SLOT), F32)
    _, acc = lax.fori_loop(0, nk, body, (m0, acc0))
    o_ref[...] = (acc / acc[:, MLA_V:MLA_V + 1]).astype(BF16)


def _flash(q, k, v, tq=512, tk=512):
    B, S, wide = q.shape
    H = wide // MLA_SLOT
    qspec = pl.BlockSpec((None, tq, MLA_SLOT), lambda b, h, i: (b, i, h))
    kvspec = pl.BlockSpec((None, S, MLA_SLOT), lambda b, h, i: (b, 0, h))
    return pl.pallas_call(
        functools.partial(_flash_kernel, tk=tk),
        out_shape=jax.ShapeDtypeStruct((B, S, wide), BF16),
        grid=(B, H, S // tq),
        in_specs=[qspec, kvspec, kvspec],
        out_specs=qspec,
        compiler_params=_params("parallel", "parallel", "arbitrary"), name="mla_flash",
    )(q, k, v)


def _router_kernel(x_ref, mods_ref, g_ref, wrt_ref, br_ref, hn_ref, gt_ref):
    hn = _modnorm(x_ref[...], g_ref[...], mods_ref[4:5, :], mods_ref[3:4, :])
    hn_ref[...] = hn.astype(BF16)
    tm = hn.shape[0]
    logits = lax.dot_general(wrt_ref[...], hn, NT_DIMS, preferred_element_type=F32,
                             precision=lax.Precision.HIGHEST)
    scores = _sigmoid(logits)
    choice = scores + br_ref[...]
    per = N_EXPERTS // N_GROUPS
    sub8 = lax.broadcasted_iota(jnp.int32, (per, tm), 0).astype(F32)
    gs = []
    for gi in range(N_GROUPS):
        cg = choice[gi * per:(gi + 1) * per, :]
        m1 = jnp.max(cg, axis=0, keepdims=True)
        i1 = jnp.min(jnp.where(cg == m1, sub8, float(per)), axis=0, keepdims=True)
        m2 = jnp.max(jnp.where(sub8 == i1, -jnp.inf, cg), axis=0, keepdims=True)
        gs.append(m1 + m2)
    gs8 = jnp.concatenate(gs, axis=0)
    rank = jnp.zeros_like(gs8)
    for gi in range(N_GROUPS):
        row = gs8[gi:gi + 1, :]
        ahead = (row > gs8) | ((row == gs8) & (sub8 > float(gi)))
        rank = rank + jnp.where(ahead, 1.0, 0.0)
    cur = jnp.concatenate(
        [jnp.where(rank[gi:gi + 1, :] < float(TOPK_GROUPS),
                   choice[gi * per:(gi + 1) * per, :], -jnp.inf)
         for gi in range(N_GROUPS)], axis=0)
    sub64 = lax.broadcasted_iota(jnp.int32, (N_EXPERTS, tm), 0).astype(F32)
    chosen = jnp.zeros_like(cur)
    for _ in range(TOP_K):
        m = jnp.max(cur, axis=0, keepdims=True)
        idx = jnp.min(jnp.where(cur == m, sub64, float(N_EXPERTS)), axis=0, keepdims=True)
        hit = sub64 == idx
        chosen = jnp.where(hit, 1.0, chosen)
        cur = jnp.where(hit, -jnp.inf, cur)
    w = chosen * scores
    gates = w / jnp.sum(w, axis=0, keepdims=True) * ROUTED_SCALE
    extra = jnp.where(sub64 == 0.0, 1.0, 0.0)
    gt_ref[...] = jnp.concatenate([gates, extra], axis=0)


def _router(x, mods, g, wrt, br, tm=512):
    B, S, _ = x.shape
    tok = pl.BlockSpec((None, tm, D_MODEL), lambda b, i: (b, i, 0))
    return pl.pallas_call(
        _router_kernel,
        out_shape=(jax.ShapeDtypeStruct((B, S, D_MODEL), BF16),
                   jax.ShapeDtypeStruct((B, 2 * N_EXPERTS, S), F32)),
        grid=(B, S // tm),
        in_specs=[tok, pl.BlockSpec((None, 8, D_MODEL), lambda b, i: (b, 0, 0)),
                  pl.BlockSpec(g.shape, lambda b, i: (0, 0)),
                  pl.BlockSpec(wrt.shape, lambda b, i: (0, 0)),
                  pl.BlockSpec(br.shape, lambda b, i: (0, 0))],
        out_specs=(tok, pl.BlockSpec((None, 2 * N_EXPERTS, tm), lambda b, i: (b, 0, i))),
        compiler_params=_params("parallel", "parallel"), name="moe_router",
    )(x, mods, g, wrt, br)


def _experts_kernel(hn_ref, gates_ref, wg_ref, wu_ref, wd_ref, x_ref, mods_ref, out_ref, acc_ref):
    e = pl.program_id(2)

    @pl.when(e == 0)
    def _():
        acc_ref[...] = jnp.zeros_like(acc_ref)

    hn = hn_ref[...]
    a = jnp.dot(hn, wg_ref[...], preferred_element_type=F32)
    u = jnp.dot(hn, wu_ref[...], preferred_element_type=F32)
    lane = lax.broadcasted_iota(jnp.int32, (1, 2 * N_EXPERTS), 1)
    gcol = jnp.sum(jnp.where(lane == e, gates_ref[...], 0.0), axis=-1, keepdims=True)
    h = (a * _sigmoid(a)) * u * gcol
    acc_ref[...] += jnp.dot(h.astype(BF16), wd_ref[...], preferred_element_type=F32)

    @pl.when(e == pl.num_programs(2) - 1)
    def _():
        out_ref[...] = x_ref[...] + mods_ref[5:6, :] * acc_ref[...]


def _experts(hn, gates, wg, wu, wd, x, mods, tm=1024):
    B, S, _ = x.shape
    ne = wg.shape[0]
    tok = pl.BlockSpec((None, tm, D_MODEL), lambda b, i, e: (b, i, 0))
    return pl.pallas_call(
        _experts_kernel,
        out_shape=jax.ShapeDtypeStruct((B, S, D_MODEL), F32),
        grid=(B, S // tm, ne),
        in_specs=[tok,
                  pl.BlockSpec((None, tm, 2 * N_EXPERTS), lambda b, i, e: (b, i, 0)),
                  pl.BlockSpec((None, D_MODEL, EXPERT_FF), lambda b, i, e: (e, 0, 0)),
                  pl.BlockSpec((None, D_MODEL, EXPERT_FF), lambda b, i, e: (e, 0, 0)),
                  pl.BlockSpec((None, EXPERT_FF, D_MODEL), lambda b, i, e: (e, 0, 0)),
                  tok, pl.BlockSpec((None, 8, D_MODEL), lambda b, i, e: (b, 0, 0))],
        out_specs=tok,
        scratch_shapes=[pltpu.VMEM((tm, D_MODEL), F32)],
        compiler_params=_params("parallel", "parallel", "arbitrary"), name="moe_experts",
    )(hn, gates, wg, wu, wd, x, mods)


def _block_diag_mean(segments, width):
    p = np.zeros((width, width), np.float32)
    pos = 0
    while pos < width:
        for length, live in segments:
            if live:
                p[pos:pos + length, pos:pos + length] = 1.0 / length
            pos += length
    return jnp.asarray(p, BF16)


def _slot_layout(w, n_heads, per_head, take, slot=MLA_SLOT):
    k = w.shape[0]
    wh = w.reshape(k, n_heads, per_head)[:, :, take]
    wh = jnp.pad(wh, ((0, 0), (0, 0), (0, slot - wh.shape[-1])))
    return wh.reshape(k, n_heads * slot)


def _slot_vector(pieces):
    v = jnp.concatenate(pieces)
    v = jnp.pad(v, (0, MLA_SLOT - v.shape[0]))
    return jnp.tile(v, 2).reshape(1, 2 * MLA_SLOT)


def _rope_tables(S):
    half = MLA_ROPE // 2
    inv = 1.0 / (ROPE_THETA ** (jnp.arange(half, dtype=F32) / half))
    ang = jnp.arange(S, dtype=F32)[:, None] * inv[None, :]
    cos, sin = jnp.cos(ang), jnp.sin(ang)
    ones = jnp.ones((S, MLA_NOPE), F32)
    pad = jnp.zeros((S, MLA_SLOT - MLA_QK), F32)
    cosf = jnp.concatenate([ones, cos, cos, pad], axis=1)
    sinf = jnp.concatenate([0.0 * ones, sin, sin, pad], axis=1)
    return jnp.tile(cosf, (1, 2)), jnp.tile(sinf, (1, 2))


def _prepare(g_norm1, g_norm2, na_w_qkv, na_g_q, na_g_k, na_rpb, na_w_o,
             mla_w_down, mla_g_cq, mla_g_ckv, mla_w_uq, mla_w_ukv, mla_g_q, mla_g_k, mla_w_o,
             w_router, b_router, w_gate, w_up, w_down, ws_gate, ws_up, ws_down):
    na = dict(
        g=g_norm1[0].reshape(1, D_MODEL),
        w=na_w_qkv[0].astype(BF16),
        p=_block_diag_mean([(NA_HEAD_DIM, True)], 256),
        gq=(jnp.tile(na_g_q[0], NA_HEADS) * NA_HEAD_DIM ** -0.5).reshape(1, D_MODEL),
        gk=jnp.tile(na_g_k[0], NA_HEADS).reshape(1, D_MODEL),
        bias=_na_bias_table(na_rpb[0]),
        wo=na_w_o[0].astype(BF16),
    )
    wdn = mla_w_down[0]
    kpe_cols = jnp.pad(wdn[:, MLA_Q_RANK + MLA_KV_RANK:],
                       ((0, 0), (MLA_NOPE, MLA_SLOT - MLA_QK)))
    nope = np.arange(MLA_NOPE)
    wuq = mla_w_uq[0]
    wukv = mla_w_ukv[0]
    wo = mla_w_o[0].reshape(MLA_HEADS, MLA_V, D_MODEL)
    wo = jnp.pad(wo, ((0, 0), (0, MLA_SLOT - MLA_V), (0, 0))).reshape(MLA_HEADS * MLA_SLOT, D_MODEL)
    gq = mla_g_q[0] * MLA_QK ** -0.5
    gk = mla_g_k[0]
    zeros_nope = jnp.zeros((MLA_NOPE,), F32)
    mla = dict(
        g=g_norm1[1].reshape(1, D_MODEL),
        wd=jnp.concatenate([wdn[:, :MLA_Q_RANK + MLA_KV_RANK], kpe_cols], axis=1).astype(BF16),
        gcq=mla_g_cq[0].reshape(1, MLA_Q_RANK),
        gckv=mla_g_ckv[0].reshape(1, MLA_KV_RANK),
        wuq=_slot_layout(wuq, MLA_HEADS, MLA_QK, np.arange(MLA_QK)).astype(BF16),
        wukv=jnp.concatenate(
            [_slot_layout(wukv, MLA_HEADS, MLA_NOPE + MLA_V, nope),
             _slot_layout(wukv, MLA_HEADS, MLA_NOPE + MLA_V, MLA_NOPE + np.arange(MLA_V))],
            axis=1).astype(BF16),
        pq=_block_diag_mean([(MLA_NOPE, True), (MLA_ROPE, True), (MLA_SLOT - MLA_QK, False)], 256),
        pk=_block_diag_mean([(MLA_NOPE, True), (MLA_SLOT - MLA_NOPE, False)], 256),
        gq=_slot_vector([gq]),
        gkn=_slot_vector([gk[:MLA_NOPE]]),
        gkpe=_slot_vector([zeros_nope, gk[MLA_NOPE:]])[:, :MLA_SLOT],
        wo=wo.astype(BF16),
    )
    moe = []
    for i in range(w_router.shape[0]):
        moe.append(dict(
            g=g_norm2[i].reshape(1, D_MODEL),
            wrt=w_router[i].T,
            br=b_router[i].reshape(N_EXPERTS, 1),
            wg=jnp.concatenate([w_gate[i], ws_gate[i][None]], axis=0).astype(BF16),
            wu=jnp.concatenate([w_up[i], ws_up[i][None]], axis=0).astype(BF16),
            wd=jnp.concatenate([w_down[i], ws_down[i][None]], axis=0).astype(BF16),
        ))
    return na, mla, moe


def _moe_layer(x, mods, p):
    hn, gates_t = _router(x, mods, p["g"], p["wrt"], p["br"])
    gates = jnp.swapaxes(gates_t, 1, 2)
    return _experts(hn, gates, p["wg"], p["wu"], p["wd"], x, mods)


def _trunk(x, mods0, mods1, na, mla, moe):
    S = x.shape[1]
    q, k, v = _na_qkv(x, mods0, na["g"], na["w"], na["p"], na["gq"], na["gk"])
    o = _na_attn(q, k, v, na["bias"])
    x = _proj_res(o, na["wo"], x, mods0, 2)
    x = _moe_layer(x, mods0, moe[0])
    cosf, sinf = _rope_tables(S)
    q, k, v = _mla_proj(x, mods1, mla["g"], mla["wd"], mla["gcq"], mla["gckv"], mla["wuq"],
                        mla["wukv"], mla["pq"], mla["pk"], mla["gq"], mla["gkn"], mla["gkpe"],
                        cosf, sinf)
    o = _flash(q, k, v)
    x = _proj_res(o, mla["wo"], x, mods1, 2)
    x = _moe_layer(x, mods1, moe[1])
    return x


def kernel(x_prompt, x_sample, c_prompt, c_sample, g_norm1, g_norm2, w_ada, b_ada, na_w_qkv, na_g_q, na_g_k, na_rpb, na_w_o, mla_w_down, mla_g_cq, mla_g_ckv, mla_w_uq, mla_w_ukv, mla_g_q, mla_g_k, mla_w_o, w_router, b_router, w_gate, w_up, w_down, ws_gate, ws_up, ws_down):
    na, mla, moe = _prepare(g_norm1, g_norm2, na_w_qkv, na_g_q, na_g_k, na_rpb, na_w_o,
                            mla_w_down, mla_g_cq, mla_g_ckv, mla_w_uq, mla_w_ukv, mla_g_q, mla_g_k,
                            mla_w_o, w_router, b_router, w_gate, w_up, w_down,
                            ws_gate, ws_up, ws_down)
    bp, bs = c_prompt.shape[0], c_sample.shape[0]
    c_all = jnp.concatenate([c_prompt, c_sample], axis=0)
    c_all = jnp.pad(c_all, ((0, (-c_all.shape[0]) % 8), (0, 0)))
    mods = []
    for i in range(w_ada.shape[0]):
        m = _mods(c_all, w_ada[i], b_ada[i]).reshape(c_all.shape[0], 6, D_MODEL)
        mods.append(jnp.pad(m, ((0, 0), (0, 2), (0, 0))))
    y_prompt = _trunk(x_prompt, mods[0][:bp], mods[1][:bp], na, mla, moe)
    y_sample = _trunk(x_sample, mods[0][bp:bp + bs], mods[1][bp:bp + bs], na, mla, moe)
    return (y_prompt, y_sample)
```

```python
import functools
import math

import jax
import jax.numpy as jnp
import numpy as np
from jax import lax
from jax.experimental import pallas as pl
from jax.experimental.pallas import tpu as pltpu

F32 = jnp.float32
BF16 = jnp.bfloat16

D_MODEL = 1024
GRID_W = 64
NA_HEADS = 16
NA_HEAD_DIM = 64
NA_WIN_H = 8
NA_WIN_W = 16
NA_Q_ROWS = 4
NA_K_ROWS = 3 * NA_Q_ROWS
MLA_HEADS = 16
MLA_Q_RANK = 384
MLA_KV_RANK = 256
MLA_NOPE = 64
MLA_ROPE = 32
MLA_V = 64
MLA_QK = MLA_NOPE + MLA_ROPE
MLA_SLOT = 128
ROPE_THETA = 10000.0
N_EXPERTS = 64
TOP_K = 6
N_GROUPS = 8
TOPK_GROUPS = 4
EXPERT_FF = 256
ROUTED_SCALE = 2.5
EPS = 1e-6
NEG = -1e30
LOG2E = math.log2(math.e)
FLASH_LOGIT_BOUND = 60.0

VMEM_LIMIT = 56 * 1024 * 1024
NT_DIMS = (((1,), (1,)), ((), ()))


def _params(*sem):
    return pltpu.CompilerParams(dimension_semantics=sem, vmem_limit_bytes=VMEM_LIMIT)


def _sigmoid(x):
    return 1.0 / (1.0 + jnp.exp(-x))


def _modnorm(x, g, sc, sh):
    ms = jnp.mean(x * x, axis=-1, keepdims=True)
    return (x * lax.rsqrt(ms + EPS) * g) * (1.0 + sc) + sh


def _rmsnorm_rows(x, g):
    ms = jnp.mean(x * x, axis=-1, keepdims=True)
    return x * lax.rsqrt(ms + EPS) * g


def _segnorm(y, p, g):
    ms = jnp.dot((y * y).astype(BF16), p, preferred_element_type=F32)
    return y * lax.rsqrt(ms + EPS) * g


def _mods_kernel(c_ref, w_ref, b_ref, o_ref):
    c = c_ref[...]
    a = c * _sigmoid(c)
    o_ref[...] = jnp.dot(a, w_ref[...], preferred_element_type=F32,
                         precision=lax.Precision.HIGHEST) + b_ref[...]


def _mods(c_all, w, b):
    rows = c_all.shape[0]
    n = w.shape[1]
    tn = 1024
    return pl.pallas_call(
        _mods_kernel,
        out_shape=jax.ShapeDtypeStruct((rows, n), F32),
        grid=(n // tn,),
        in_specs=[pl.BlockSpec((rows, D_MODEL), lambda j: (0, 0)),
                  pl.BlockSpec((D_MODEL, tn), lambda j: (0, j)),
                  pl.BlockSpec((1, tn), lambda j: (0, j))],
        out_specs=pl.BlockSpec((rows, tn), lambda j: (0, j)),
        compiler_params=_params("arbitrary"), name="adaln_mods",
    )(c_all, w, b.reshape(1, n))


def _na_qkv_kernel(x_ref, mods_ref, g_ref, w_ref, p_ref, gq_ref, gk_ref, q_ref, k_ref, v_ref):
    hn = _modnorm(x_ref[...], g_ref[...], mods_ref[1:2, :], mods_ref[0:1, :]).astype(BF16)
    for part, o_ref, gg in ((0, q_ref, gq_ref), (1, k_ref, gk_ref), (2, v_ref, None)):
        for t in range(D_MODEL // 256):
            lo = t * 256
            y = jnp.dot(hn, w_ref[:, part * D_MODEL + lo:part * D_MODEL + lo + 256],
                        preferred_element_type=F32)
            if gg is not None:
                y = _segnorm(y, p_ref[...], gg[:, lo:lo + 256])
            o_ref[:, lo:lo + 256] = y.astype(BF16)


def _na_qkv(x, mods, g, w, p, gq, gk, tm=512):
    B, S, _ = x.shape
    tok = pl.BlockSpec((None, tm, D_MODEL), lambda b, i: (b, i, 0))
    full = lambda a: pl.BlockSpec(a.shape, lambda b, i: (0,) * a.ndim)
    out = jax.ShapeDtypeStruct((B, S, D_MODEL), BF16)
    return pl.pallas_call(
        _na_qkv_kernel,
        out_shape=(out, out, out),
        grid=(B, S // tm),
        in_specs=[tok, pl.BlockSpec((None, 8, D_MODEL), lambda b, i: (b, 0, 0)),
                  full(g), full(w), full(p), full(gq), full(gk)],
        out_specs=(tok, tok, tok),
        compiler_params=_params("parallel", "parallel"), name="na_qkv",
    )(x, mods, g, w, p, gq, gk)


def _na_attn_kernel(q_ref, k0_ref, k1_ref, k2_ref, v0_ref, v1_ref, v2_ref, bias_ref, o_ref):
    lane = lax.broadcasted_iota(jnp.int32, (1, 128), 1)
    first = lane < NA_HEAD_DIM
    for hp in range(NA_HEADS // 2):
        cols = slice(hp * 128, (hp + 1) * 128)
        q = q_ref[:, cols]
        k = jnp.concatenate([k0_ref[:, cols], k1_ref[:, cols], k2_ref[:, cols]], axis=0)
        v = jnp.concatenate([v0_ref[:, cols], v1_ref[:, cols], v2_ref[:, cols]], axis=0)
        outs = []
        for sub in range(2):
            keep = first if sub == 0 else jnp.logical_not(first)
            qm = jnp.where(keep, q, jnp.zeros_like(q))
            s = lax.dot_general(qm, k, NT_DIMS, preferred_element_type=F32)
            s = s + bias_ref[2 * hp + sub].astype(F32)
            m = jnp.max(s, axis=-1, keepdims=True)
            e = jnp.exp(s - m)
            l = jnp.sum(e, axis=-1, keepdims=True)
            o = jnp.dot(e.astype(BF16), v, preferred_element_type=F32)
            outs.append(o / l)
        o_ref[:, cols] = jnp.where(first, outs[0], outs[1]).astype(BF16)


def _na_attn(q, k, v, bias):
    B, S, _ = q.shape
    tq = NA_Q_ROWS * GRID_W
    ng = S // tq
    assert ng >= 3

    def kv_spec(j):
        return pl.BlockSpec((None, tq, D_MODEL),
                            lambda b, g: (b, jnp.clip(g - 1, 0, ng - 3) + j, 0))

    def bias_map(b, g):
        return (0, jnp.where(g == 0, 0, jnp.where(g == ng - 1, 2, 1)), 0)

    qspec = pl.BlockSpec((None, tq, D_MODEL), lambda b, g: (b, g, 0))
    return pl.pallas_call(
        _na_attn_kernel,
        out_shape=jax.ShapeDtypeStruct((B, S, D_MODEL), BF16),
        grid=(B, ng),
        in_specs=[qspec, kv_spec(0), kv_spec(1), kv_spec(2), kv_spec(0), kv_spec(1), kv_spec(2),
                  pl.BlockSpec((NA_HEADS, tq, 3 * tq), bias_map)],
        out_specs=qspec,
        compiler_params=_params("parallel", "arbitrary"), name="na_attn",
    )(q, k, k, k, v, v, v, bias)


def _na_bias_table(rpb):
    rows = NA_K_ROWS
    r = np.arange(rows)
    c = np.arange(GRID_W)
    rs = np.clip(r - NA_WIN_H // 2, 0, rows - NA_WIN_H)
    cs = np.clip(c - NA_WIN_W // 2, 0, GRID_W - NA_WIN_W)
    dr = r[None, :] - r[:, None]
    dc = c[None, :] - c[:, None]
    ok_r = (r[None, :] >= rs[:, None]) & (r[None, :] < rs[:, None] + NA_WIN_H)
    ok_c = (c[None, :] >= cs[:, None]) & (c[None, :] < cs[:, None] + NA_WIN_W)
    ri = np.clip(dr + NA_WIN_H - 1, 0, 2 * NA_WIN_H - 2)
    ci = np.clip(dc + NA_WIN_W - 1, 0, 2 * NA_WIN_W - 2)
    ok = ok_r[:, None, :, None] & ok_c[None, :, None, :]
    sel_r = jnp.asarray(ri[:, :, None] == np.arange(2 * NA_WIN_H - 1), F32)
    sel_c = jnp.asarray(ci[:, :, None] == np.arange(2 * NA_WIN_W - 1), F32)
    by_row = jnp.einsum('hij,qki->hqkj', rpb, sel_r, precision=lax.Precision.HIGHEST)
    tab = jnp.einsum('hqkj,cdj->hqckd', by_row, sel_c, precision=lax.Precision.HIGHEST)
    tab = jnp.where(ok[None], tab, NEG)
    return tab.reshape(NA_HEADS, rows * GRID_W, rows * GRID_W).astype(BF16)


def _proj_res_kernel(o_ref, w_ref, x_ref, mods_ref, out_ref, *, gate_row):
    y = jnp.dot(o_ref[...], w_ref[...], preferred_element_type=F32)
    out_ref[...] = x_ref[...] + mods_ref[gate_row:gate_row + 1, :] * y


def _proj_res(o, w, x, mods, gate_row, tm=512):
    B, S, K = o.shape
    tok = pl.BlockSpec((None, tm, D_MODEL), lambda b, i: (b, i, 0))
    return pl.pallas_call(
        functools.partial(_proj_res_kernel, gate_row=gate_row),
        out_shape=jax.ShapeDtypeStruct((B, S, D_MODEL), F32),
        grid=(B, S // tm),
        in_specs=[pl.BlockSpec((None, tm, K), lambda b, i: (b, i, 0)),
                  pl.BlockSpec(w.shape, lambda b, i: (0, 0)),
                  tok, pl.BlockSpec((None, 8, D_MODEL), lambda b, i: (b, 0, 0))],
        out_specs=tok,
        compiler_params=_params("parallel", "parallel"), name="proj_res",
    )(o, w, x, mods)


def _rope(y, cosf, sinf, width):
    lane = lax.broadcasted_iota(jnp.int32, (1, width), 1) % MLA_SLOT
    half = MLA_ROPE // 2
    up = pltpu.roll(y, width - half, 1)
    dn = pltpu.roll(y, half, 1)
    lo = (lane >= MLA_NOPE) & (lane < MLA_NOPE + half)
    hi = (lane >= MLA_NOPE + half) & (lane < MLA_QK)
    rot = jnp.where(lo, -up, jnp.where(hi, dn, 0.0))
    return y * cosf + rot * sinf


def _mla_proj_kernel(x_ref, mods_ref, g_ref, wd_ref, gcq_ref, gckv_ref, wuq_ref, wukv_ref,
                     pq_ref, pk_ref, gq_ref, gkn_ref, gkpe_ref, cos_ref, sin_ref,
                     q_ref, k_ref, v_ref):
    hn = _modnorm(x_ref[...], g_ref[...], mods_ref[1:2, :], mods_ref[0:1, :]).astype(BF16)
    down = jnp.dot(hn, wd_ref[...], preferred_element_type=F32)
    cq = _rmsnorm_rows(down[:, :MLA_Q_RANK], gcq_ref[...]).astype(BF16)
    ckv = _rmsnorm_rows(down[:, MLA_Q_RANK:MLA_Q_RANK + MLA_KV_RANK], gckv_ref[...]).astype(BF16)
    cosf = cos_ref[...]
    sinf = sin_ref[...]
    kpe = down[:, MLA_Q_RANK + MLA_KV_RANK:]
    kpe = _segnorm(kpe, pq_ref[:MLA_SLOT, :MLA_SLOT], gkpe_ref[...])
    kpe = _rope(kpe, cosf[:, :MLA_SLOT], sinf[:, :MLA_SLOT], MLA_SLOT)
    kpe2 = jnp.concatenate([kpe, kpe], axis=1)
    lane = lax.broadcasted_iota(jnp.int32, (1, 256), 1) % MLA_SLOT
    ones_lane = jnp.where(lane == MLA_V, 1.0, 0.0)
    n_tiles = MLA_HEADS * MLA_SLOT // 256
    for t in range(n_tiles):
        cols = slice(t * 256, (t + 1) * 256)
        y = jnp.dot(cq, wuq_ref[:, cols], preferred_element_type=F32)
        y = _segnorm(y, pq_ref[...], gq_ref[...])
        q_ref[:, cols] = _rope(y, cosf, sinf, 256).astype(BF16)
    for t in range(n_tiles):
        cols = slice(t * 256, (t + 1) * 256)
        y = jnp.dot(ckv, wukv_ref[:, cols], preferred_element_type=F32)
        y = _segnorm(y, pk_ref[...], gkn_ref[...])
        k_ref[:, cols] = (y + kpe2).astype(BF16)
    off = MLA_HEADS * MLA_SLOT
    for t in range(n_tiles):
        cols = slice(t * 256, (t + 1) * 256)
        y = jnp.dot(ckv, wukv_ref[:, off + t * 256:off + (t + 1) * 256], preferred_element_type=F32)
        v_ref[:, cols] = (y + ones_lane).astype(BF16)


def _mla_proj(x, mods, g, wd, gcq, gckv, wuq, wukv, pq, pk, gq, gkn, gkpe, cosf, sinf, tm=512):
    B, S, _ = x.shape
    wide = MLA_HEADS * MLA_SLOT
    tok = pl.BlockSpec((None, tm, D_MODEL), lambda b, i: (b, i, 0))
    full = lambda a: pl.BlockSpec(a.shape, lambda b, i: (0,) * a.ndim)
    pos = pl.BlockSpec((tm, 256), lambda b, i: (i, 0))
    out = jax.ShapeDtypeStruct((B, S, wide), BF16)
    ospec = pl.BlockSpec((None, tm, wide), lambda b, i: (b, i, 0))
    return pl.pallas_call(
        _mla_proj_kernel,
        out_shape=(out, out, out),
        grid=(B, S // tm),
        in_specs=[tok, pl.BlockSpec((None, 8, D_MODEL), lambda b, i: (b, 0, 0)),
                  full(g), full(wd), full(gcq), full(gckv), full(wuq), full(wukv),
                  full(pq), full(pk), full(gq), full(gkn), full(gkpe), pos, pos],
        out_specs=(ospec, ospec, ospec),
        compiler_params=_params("parallel", "parallel"), name="mla_proj",
    )(x, mods, g, wd, gcq, gckv, wuq, wukv, pq, pk, gq, gkn, gkpe, cosf, sinf)


def _flash_kernel(q_ref, k_ref, v_ref, o_ref, *, tk):
    q = q_ref[...]
    tq = q.shape[0]
    nk = k_ref.shape[0] // tk

    def body(j, carry):
        m, acc = carry
        ks = pl.multiple_of(j * tk, tk)
        k = k_ref[pl.ds(ks, tk), :]
        v = v_ref[pl.ds(ks, tk), :]
        s = lax.dot_general(q, k, NT_DIMS, preferred_element_type=F32)
        m_new = jnp.maximum(m, jnp.max(s, axis=-1, keepdims=True))
        alpha = jnp.exp2(m - m_new)
        p = jnp.exp2(s - m_new)
        acc = alpha * acc + jnp.dot(p.astype(BF16), v, preferred_element_type=F32)
        return m_new, acc

    m0 = jnp.full((tq, 1), -jnp.inf, F32)
    acc0 = jnp.zeros((tq, MLA_SLOT), F32)
    _, acc = lax.fori_loop(0, nk, body, (m0, acc0))
    o_ref[...] = (acc / acc[:, MLA_V:MLA_V + 1]).astype(BF16)


def _flash_bounded_kernel(q_ref, k_ref, v_ref, o_ref, acc_ref, *, tk, sub):
    tq = q_ref.shape[0]
    nk = k_ref.shape[0] // tk
    acc_ref[...] = jnp.zeros_like(acc_ref)

    def body(j, carry):
        ks = pl.multiple_of(j * tk, tk)
        k = k_ref[pl.ds(ks, tk), :]
        v = v_ref[pl.ds(ks, tk), :]
        for a in range(tq // sub):
            rows = slice(a * sub, (a + 1) * sub)
            s = lax.dot_general(q_ref[rows, :], k, NT_DIMS, preferred_element_type=F32)
            acc_ref[rows, :] += jnp.dot(jnp.exp2(s).astype(BF16), v, preferred_element_type=F32)
        return carry

    lax.fori_loop(0, nk, body, 0, unroll=2)
    acc = acc_ref[...]
    o_ref[...] = (acc / acc[:, MLA_V:MLA_V + 1]).astype(BF16)


def _flash_call(kernel_fn, q, k, v, tq, scratch, name):
    B, S, wide = q.shape
    H = wide // MLA_SLOT
    qspec = pl.BlockSpec((None, tq, MLA_SLOT), lambda b, h, i: (b, i, h))
    kvspec = pl.BlockSpec((None, S, MLA_SLOT), lambda b, h, i: (b, 0, h))
    return pl.pallas_call(
        kernel_fn,
        out_shape=jax.ShapeDtypeStruct((B, S, wide), BF16),
        grid=(B, H, S // tq),
        in_specs=[qspec, kvspec, kvspec],
        out_specs=qspec,
        scratch_shapes=scratch,
        compiler_params=_params("parallel", "parallel", "arbitrary"), name=name,
    )(q, k, v)


def _flash(q, k, v, bounded):
    fast = functools.partial(
        _flash_call, functools.partial(_flash_bounded_kernel, tk=1024, sub=256),
        tq=1024, scratch=[pltpu.VMEM((1024, MLA_SLOT), F32)], name="mla_flash_bounded")
    general = functools.partial(
        _flash_call, functools.partial(_flash_kernel, tk=512),
        tq=512, scratch=[], name="mla_flash")
    return lax.cond(bounded, fast, general, q, k, v)


def _mla_logit_bound(g_q, g_k):
    def seg(g):
        return MLA_NOPE * jnp.max(jnp.abs(g[:MLA_NOPE])) ** 2 + MLA_ROPE * jnp.max(jnp.abs(g[MLA_NOPE:])) ** 2
    return 1.05 * jnp.sqrt(seg(g_q) * seg(g_k)) * MLA_QK ** -0.5 * LOG2E


def _router_kernel(x_ref, mods_ref, g_ref, wrt_ref, br_ref, hn_ref, gt_ref):
    hn = _modnorm(x_ref[...], g_ref[...], mods_ref[4:5, :], mods_ref[3:4, :])
    hn_ref[...] = hn.astype(BF16)
    tm = hn.shape[0]
    logits = lax.dot_general(wrt_ref[...], hn, NT_DIMS, preferred_element_type=F32,
                             precision=lax.Precision.HIGHEST)
    scores = _sigmoid(logits)
    choice = scores + br_ref[...]
    per = N_EXPERTS // N_GROUPS
    sub8 = lax.broadcasted_iota(jnp.int32, (per, tm), 0).astype(F32)
    gs = []
    for gi in range(N_GROUPS):
        cg = choice[gi * per:(gi + 1) * per, :]
        m1 = jnp.max(cg, axis=0, keepdims=True)
        i1 = jnp.min(jnp.where(cg == m1, sub8, float(per)), axis=0, keepdims=True)
        m2 = jnp.max(jnp.where(sub8 == i1, -jnp.inf, cg), axis=0, keepdims=True)
        gs.append(m1 + m2)
    gs8 = jnp.concatenate(gs, axis=0)
    rank = jnp.zeros_like(gs8)
    for gi in range(N_GROUPS):
        row = gs8[gi:gi + 1, :]
        ahead = (row > gs8) | ((row == gs8) & (sub8 > float(gi)))
        rank = rank + jnp.where(ahead, 1.0, 0.0)
    cur = jnp.concatenate(
        [jnp.where(rank[gi:gi + 1, :] < float(TOPK_GROUPS),
                   choice[gi * per:(gi + 1) * per, :], -jnp.inf)
         for gi in range(N_GROUPS)], axis=0)
    sub64 = lax.broadcasted_iota(jnp.int32, (N_EXPERTS, tm), 0).astype(F32)
    chosen = jnp.zeros_like(cur)
    for _ in range(TOP_K):
        m = jnp.max(cur, axis=0, keepdims=True)
        idx = jnp.min(jnp.where(cur == m, sub64, float(N_EXPERTS)), axis=0, keepdims=True)
        hit = sub64 == idx
        chosen = jnp.where(hit, 1.0, chosen)
        cur = jnp.where(hit, -jnp.inf, cur)
    w = chosen * scores
    gates = w / jnp.sum(w, axis=0, keepdims=True) * ROUTED_SCALE
    extra = jnp.where(sub64 == 0.0, 1.0, 0.0)
    gt_ref[...] = jnp.concatenate([gates, extra], axis=0)


def _router(x, mods, g, wrt, br, tm=512):
    B, S, _ = x.shape
    tok = pl.BlockSpec((None, tm, D_MODEL), lambda b, i: (b, i, 0))
    return pl.pallas_call(
        _router_kernel,
        out_shape=(jax.ShapeDtypeStruct((B, S, D_MODEL), BF16),
                   jax.ShapeDtypeStruct((B, 2 * N_EXPERTS, S), F32)),
        grid=(B, S // tm),
        in_specs=[tok, pl.BlockSpec((None, 8, D_MODEL), lambda b, i: (b, 0, 0)),
                  pl.BlockSpec(g.shape, lambda b, i: (0, 0)),
                  pl.BlockSpec(wrt.shape, lambda b, i: (0, 0)),
                  pl.BlockSpec(br.shape, lambda b, i: (0, 0))],
        out_specs=(tok, pl.BlockSpec((None, 2 * N_EXPERTS, tm), lambda b, i: (b, 0, i))),
        compiler_params=_params("parallel", "parallel"), name="moe_router",
    )(x, mods, g, wrt, br)


def _experts_kernel(hn_ref, gates_ref, wg_ref, wu_ref, wd_ref, x_ref, mods_ref, out_ref, acc_ref):
    e = pl.program_id(2)

    @pl.when(e == 0)
    def _():
        acc_ref[...] = jnp.zeros_like(acc_ref)

    hn = hn_ref[...]
    a = jnp.dot(hn, wg_ref[...], preferred_element_type=F32)
    u = jnp.dot(hn, wu_ref[...], preferred_element_type=F32)
    lane = lax.broadcasted_iota(jnp.int32, (1, 2 * N_EXPERTS), 1)
    gcol = jnp.sum(jnp.where(lane == e, gates_ref[...], 0.0), axis=-1, keepdims=True)
    h = (a * _sigmoid(a)) * u * gcol
    acc_ref[...] += jnp.dot(h.astype(BF16), wd_ref[...], preferred_element_type=F32)

    @pl.when(e == pl.num_programs(2) - 1)
    def _():
        out_ref[...] = x_ref[...] + mods_ref[5:6, :] * acc_ref[...]


def _experts(hn, gates, wg, wu, wd, x, mods, tm=1024):
    B, S, _ = x.shape
    ne = wg.shape[0]
    tok = pl.BlockSpec((None, tm, D_MODEL), lambda b, i, e: (b, i, 0))
    return pl.pallas_call(
        _experts_kernel,
        out_shape=jax.ShapeDtypeStruct((B, S, D_MODEL), F32),
        grid=(B, S // tm, ne),
        in_specs=[tok,
                  pl.BlockSpec((None, tm, 2 * N_EXPERTS), lambda b, i, e: (b, i, 0)),
                  pl.BlockSpec((None, D_MODEL, EXPERT_FF), lambda b, i, e: (e, 0, 0)),
                  pl.BlockSpec((None, D_MODEL, EXPERT_FF), lambda b, i, e: (e, 0, 0)),
                  pl.BlockSpec((None, EXPERT_FF, D_MODEL), lambda b, i, e: (e, 0, 0)),
                  tok, pl.BlockSpec((None, 8, D_MODEL), lambda b, i, e: (b, 0, 0))],
        out_specs=tok,
        scratch_shapes=[pltpu.VMEM((tm, D_MODEL), F32)],
        compiler_params=_params("parallel", "parallel", "arbitrary"), name="moe_experts",
    )(hn, gates, wg, wu, wd, x, mods)


def _block_diag_mean(segments, width):
    p = np.zeros((width, width), np.float32)
    pos = 0
    while pos < width:
        for length, live in segments:
            if live:
                p[pos:pos + length, pos:pos + length] = 1.0 / length
            pos += length
    return jnp.asarray(p, BF16)


def _slot_layout(w, n_heads, per_head, take, slot=MLA_SLOT):
    k = w.shape[0]
    wh = w.reshape(k, n_heads, per_head)[:, :, take]
    wh = jnp.pad(wh, ((0, 0), (0, 0), (0, slot - wh.shape[-1])))
    return wh.reshape(k, n_heads * slot)


def _slot_vector(pieces):
    v = jnp.concatenate(pieces)
    v = jnp.pad(v, (0, MLA_SLOT - v.shape[0]))
    return jnp.tile(v, 2).reshape(1, 2 * MLA_SLOT)


def _rope_tables(S):
    half = MLA_ROPE // 2
    inv = 1.0 / (ROPE_THETA ** (jnp.arange(half, dtype=F32) / half))
    ang = jnp.arange(S, dtype=F32)[:, None] * inv[None, :]
    cos, sin = jnp.cos(ang), jnp.sin(ang)
    ones = jnp.ones((S, MLA_NOPE), F32)
    pad = jnp.zeros((S, MLA_SLOT - MLA_QK), F32)
    cosf = jnp.concatenate([ones, cos, cos, pad], axis=1)
    sinf = jnp.concatenate([0.0 * ones, sin, sin, pad], axis=1)
    return jnp.tile(cosf, (1, 2)), jnp.tile(sinf, (1, 2))


def _prepare(g_norm1, g_norm2, na_w_qkv, na_g_q, na_g_k, na_rpb, na_w_o,
             mla_w_down, mla_g_cq, mla_g_ckv, mla_w_uq, mla_w_ukv, mla_g_q, mla_g_k, mla_w_o,
             w_router, b_router, w_gate, w_up, w_down, ws_gate, ws_up, ws_down):
    na = dict(
        g=g_norm1[0].reshape(1, D_MODEL),
        w=na_w_qkv[0].astype(BF16),
        p=_block_diag_mean([(NA_HEAD_DIM, True)], 256),
        gq=(jnp.tile(na_g_q[0], NA_HEADS) * NA_HEAD_DIM ** -0.5).reshape(1, D_MODEL),
        gk=jnp.tile(na_g_k[0], NA_HEADS).reshape(1, D_MODEL),
        bias=_na_bias_table(na_rpb[0]),
        wo=na_w_o[0].astype(BF16),
    )
    wdn = mla_w_down[0]
    kpe_cols = jnp.pad(wdn[:, MLA_Q_RANK + MLA_KV_RANK:],
                       ((0, 0), (MLA_NOPE, MLA_SLOT - MLA_QK)))
    nope = np.arange(MLA_NOPE)
    wuq = mla_w_uq[0]
    wukv = mla_w_ukv[0]
    wo = mla_w_o[0].reshape(MLA_HEADS, MLA_V, D_MODEL)
    wo = jnp.pad(wo, ((0, 0), (0, MLA_SLOT - MLA_V), (0, 0))).reshape(MLA_HEADS * MLA_SLOT, D_MODEL)
    gq = mla_g_q[0] * (MLA_QK ** -0.5 * LOG2E)
    gk = mla_g_k[0]
    zeros_nope = jnp.zeros((MLA_NOPE,), F32)
    mla = dict(
        g=g_norm1[1].reshape(1, D_MODEL),
        wd=jnp.concatenate([wdn[:, :MLA_Q_RANK + MLA_KV_RANK], kpe_cols], axis=1).astype(BF16),
        gcq=mla_g_cq[0].reshape(1, MLA_Q_RANK),
        gckv=mla_g_ckv[0].reshape(1, MLA_KV_RANK),
        wuq=_slot_layout(wuq, MLA_HEADS, MLA_QK, np.arange(MLA_QK)).astype(BF16),
        wukv=jnp.concatenate(
            [_slot_layout(wukv, MLA_HEADS, MLA_NOPE + MLA_V, nope),
             _slot_layout(wukv, MLA_HEADS, MLA_NOPE + MLA_V, MLA_NOPE + np.arange(MLA_V))],
            axis=1).astype(BF16),
        pq=_block_diag_mean([(MLA_NOPE, True), (MLA_ROPE, True), (MLA_SLOT - MLA_QK, False)], 256),
        pk=_block_diag_mean([(MLA_NOPE, True), (MLA_SLOT - MLA_NOPE, False)], 256),
        gq=_slot_vector([gq]),
        gkn=_slot_vector([gk[:MLA_NOPE]]),
        gkpe=_slot_vector([zeros_nope, gk[MLA_NOPE:]])[:, :MLA_SLOT],
        wo=wo.astype(BF16),
        bounded=_mla_logit_bound(mla_g_q[0], mla_g_k[0]) < FLASH_LOGIT_BOUND,
    )
    moe = []
    for i in range(w_router.shape[0]):
        moe.append(dict(
            g=g_norm2[i].reshape(1, D_MODEL),
            wrt=w_router[i].T,
            br=b_router[i].reshape(N_EXPERTS, 1),
            wg=jnp.concatenate([w_gate[i], ws_gate[i][None]], axis=0).astype(BF16),
            wu=jnp.concatenate([w_up[i], ws_up[i][None]], axis=0).astype(BF16),
            wd=jnp.concatenate([w_down[i], ws_down[i][None]], axis=0).astype(BF16),
        ))
    return na, mla, moe


def _moe_layer(x, mods, p):
    hn, gates_t = _router(x, mods, p["g"], p["wrt"], p["br"])
    gates = jnp.swapaxes(gates_t, 1, 2)
    return _experts(hn, gates, p["wg"], p["wu"], p["wd"], x, mods)


def _trunk(x, mods0, mods1, na, mla, moe):
    S = x.shape[1]
    q, k, v = _na_qkv(x, mods0, na["g"], na["w"], na["p"], na["gq"], na["gk"])
    o = _na_attn(q, k, v, na["bias"])
    x = _proj_res(o, na["wo"], x, mods0, 2)
    x = _moe_layer(x, mods0, moe[0])
    cosf, sinf = _rope_tables(S)
    q, k, v = _mla_proj(x, mods1, mla["g"], mla["wd"], mla["gcq"], mla["gckv"], mla["wuq"],
                        mla["wukv"], mla["pq"], mla["pk"], mla["gq"], mla["gkn"], mla["gkpe"],
                        cosf, sinf)
    o = _flash(q, k, v, mla["bounded"])
    x = _proj_res(o, mla["wo"], x, mods1, 2)
    x = _moe_layer(x, mods1, moe[1])
    return x


def kernel(x_prompt, x_sample, c_prompt, c_sample, g_norm1, g_norm2, w_ada, b_ada, na_w_qkv, na_g_q, na_g_k, na_rpb, na_w_o, mla_w_down, mla_g_cq, mla_g_ckv, mla_w_uq, mla_w_ukv, mla_g_q, mla_g_k, mla_w_o, w_router, b_router, w_gate, w_up, w_down, ws_gate, ws_up, ws_down):
    na, mla, moe = _prepare(g_norm1, g_norm2, na_w_qkv, na_g_q, na_g_k, na_rpb, na_w_o,
                            mla_w_down, mla_g_cq, mla_g_ckv, mla_w_uq, mla_w_ukv, mla_g_q, mla_g_k,
                            mla_w_o, w_router, b_router, w_gate, w_up, w_down,
                            ws_gate, ws_up, ws_down)
    bp, bs = c_prompt.shape[0], c_sample.shape[0]
    c_all = jnp.concatenate([c_prompt, c_sample], axis=0)
    c_all = jnp.pad(c_all, ((0, (-c_all.shape[0]) % 8), (0, 0)))
    mods = []
    for i in range(w_ada.shape[0]):
        m = _mods(c_all, w_ada[i], b_ada[i]).reshape(c_all.shape[0], 6, D_MODEL)
        mods.append(jnp.pad(m, ((0, 0), (0, 2), (0, 0))))
    y_prompt = _trunk(x_prompt, mods[0][:bp], mods[1][:bp], na, mla, moe)
    y_sample = _trunk(x_sample, mods[0][bp:bp + bs], mods[1][bp:bp + bs], na, mla, moe)
    return (y_prompt, y_sample)
```

```python
import functools
import math

import jax
import jax.numpy as jnp
import numpy as np
from jax import lax
from jax.experimental import pallas as pl
from jax.experimental.pallas import tpu as pltpu
from jax.experimental.pallas import tpu_sc as plsc

F32 = jnp.float32
BF16 = jnp.bfloat16

D_MODEL = 1024
GRID_W = 64
NA_HEADS = 16
NA_HEAD_DIM = 64
NA_WIN_H = 8
NA_WIN_W = 16
NA_Q_ROWS = 4
NA_K_ROWS = 3 * NA_Q_ROWS
MLA_HEADS = 16
MLA_Q_RANK = 384
MLA_KV_RANK = 256
MLA_NOPE = 64
MLA_ROPE = 32
MLA_V = 64
MLA_QK = MLA_NOPE + MLA_ROPE
MLA_SLOT = 128
ROPE_THETA = 10000.0
N_EXPERTS = 64
TOP_K = 6
N_GROUPS = 8
TOPK_GROUPS = 4
EXPERT_FF = 256
ROUTED_SCALE = 2.5
EXPERT_TILE = 512
SC_CORES = 2
SC_SUBCORES = 16
SC_WORKERS = SC_CORES * SC_SUBCORES
SC_GATHER_ROWS = 64
HIGH_HALF = -65536
EPS = 1e-6
NEG = -1e30
LOG2E = math.log2(math.e)
FLASH_LOGIT_BOUND = 60.0

VMEM_LIMIT = 56 * 1024 * 1024
NT_DIMS = (((1,), (1,)), ((), ()))


def _params(*sem):
    return pltpu.CompilerParams(dimension_semantics=sem, vmem_limit_bytes=VMEM_LIMIT)


def _sigmoid(x):
    return 1.0 / (1.0 + jnp.exp(-x))


def _modnorm(x, g, sc, sh):
    ms = jnp.mean(x * x, axis=-1, keepdims=True)
    return (x * lax.rsqrt(ms + EPS) * g) * (1.0 + sc) + sh


def _rmsnorm_rows(x, g):
    ms = jnp.mean(x * x, axis=-1, keepdims=True)
    return x * lax.rsqrt(ms + EPS) * g


def _segnorm(y, p, g):
    ms = jnp.dot((y * y).astype(BF16), p, preferred_element_type=F32)
    return y * lax.rsqrt(ms + EPS) * g


def _mods_kernel(c_ref, w_ref, b_ref, o_ref):
    c = c_ref[...]
    a = c * _sigmoid(c)
    o_ref[...] = jnp.dot(a, w_ref[...], preferred_element_type=F32,
                         precision=lax.Precision.HIGHEST) + b_ref[...]


def _mods(c_all, w, b):
    rows = c_all.shape[0]
    n = w.shape[1]
    tn = 1024
    return pl.pallas_call(
        _mods_kernel,
        out_shape=jax.ShapeDtypeStruct((rows, n), F32),
        grid=(n // tn,),
        in_specs=[pl.BlockSpec((rows, D_MODEL), lambda j: (0, 0)),
                  pl.BlockSpec((D_MODEL, tn), lambda j: (0, j)),
                  pl.BlockSpec((1, tn), lambda j: (0, j))],
        out_specs=pl.BlockSpec((rows, tn), lambda j: (0, j)),
        compiler_params=_params("arbitrary"), name="adaln_mods",
    )(c_all, w, b.reshape(1, n))


def _na_qkv_kernel(x_ref, mods_ref, g_ref, w_ref, p_ref, gq_ref, gk_ref, q_ref, k_ref, v_ref):
    hn = _modnorm(x_ref[...], g_ref[...], mods_ref[1:2, :], mods_ref[0:1, :]).astype(BF16)
    for part, o_ref, gg in ((0, q_ref, gq_ref), (1, k_ref, gk_ref), (2, v_ref, None)):
        for t in range(D_MODEL // 256):
            lo = t * 256
            y = jnp.dot(hn, w_ref[:, part * D_MODEL + lo:part * D_MODEL + lo + 256],
                        preferred_element_type=F32)
            if gg is not None:
                y = _segnorm(y, p_ref[...], gg[:, lo:lo + 256])
            o_ref[:, lo:lo + 256] = y.astype(BF16)


def _na_qkv(x, mods, g, w, p, gq, gk, tm=512):
    B, S, _ = x.shape
    tok = pl.BlockSpec((None, tm, D_MODEL), lambda b, i: (b, i, 0))
    full = lambda a: pl.BlockSpec(a.shape, lambda b, i: (0,) * a.ndim)
    out = jax.ShapeDtypeStruct((B, S, D_MODEL), BF16)
    return pl.pallas_call(
        _na_qkv_kernel,
        out_shape=(out, out, out),
        grid=(B, S // tm),
        in_specs=[tok, pl.BlockSpec((None, 8, D_MODEL), lambda b, i: (b, 0, 0)),
                  full(g), full(w), full(p), full(gq), full(gk)],
        out_specs=(tok, tok, tok),
        compiler_params=_params("parallel", "parallel"), name="na_qkv",
    )(x, mods, g, w, p, gq, gk)


def _na_attn_kernel(q_ref, k0_ref, k1_ref, k2_ref, v0_ref, v1_ref, v2_ref, bias_ref, o_ref):
    lane = lax.broadcasted_iota(jnp.int32, (1, 128), 1)
    first = lane < NA_HEAD_DIM
    for hp in range(NA_HEADS // 2):
        cols = slice(hp * 128, (hp + 1) * 128)
        q = q_ref[:, cols]
        k = jnp.concatenate([k0_ref[:, cols], k1_ref[:, cols], k2_ref[:, cols]], axis=0)
        v = jnp.concatenate([v0_ref[:, cols], v1_ref[:, cols], v2_ref[:, cols]], axis=0)
        outs = []
        for sub in range(2):
            keep = first if sub == 0 else jnp.logical_not(first)
            qm = jnp.where(keep, q, jnp.zeros_like(q))
            s = lax.dot_general(qm, k, NT_DIMS, preferred_element_type=F32)
            s = s + bias_ref[2 * hp + sub].astype(F32)
            m = jnp.max(s, axis=-1, keepdims=True)
            e = jnp.exp(s - m)
            l = jnp.sum(e, axis=-1, keepdims=True)
            o = jnp.dot(e.astype(BF16), v, preferred_element_type=F32)
            outs.append(o / l)
        o_ref[:, cols] = jnp.where(first, outs[0], outs[1]).astype(BF16)


def _na_attn(q, k, v, bias):
    B, S, _ = q.shape
    tq = NA_Q_ROWS * GRID_W
    ng = S // tq
    assert ng >= 3

    def kv_spec(j):
        return pl.BlockSpec((None, tq, D_MODEL),
                            lambda b, g: (b, jnp.clip(g - 1, 0, ng - 3) + j, 0))

    def bias_map(b, g):
        return (0, jnp.where(g == 0, 0, jnp.where(g == ng - 1, 2, 1)), 0)

    qspec = pl.BlockSpec((None, tq, D_MODEL), lambda b, g: (b, g, 0))
    return pl.pallas_call(
        _na_attn_kernel,
        out_shape=jax.ShapeDtypeStruct((B, S, D_MODEL), BF16),
        grid=(B, ng),
        in_specs=[qspec, kv_spec(0), kv_spec(1), kv_spec(2), kv_spec(0), kv_spec(1), kv_spec(2),
                  pl.BlockSpec((NA_HEADS, tq, 3 * tq), bias_map)],
        out_specs=qspec,
        compiler_params=_params("parallel", "arbitrary"), name="na_attn",
    )(q, k, k, k, v, v, v, bias)


def _na_bias_table(rpb):
    rows = NA_K_ROWS
    r = np.arange(rows)
    c = np.arange(GRID_W)
    rs = np.clip(r - NA_WIN_H // 2, 0, rows - NA_WIN_H)
    cs = np.clip(c - NA_WIN_W // 2, 0, GRID_W - NA_WIN_W)
    dr = r[None, :] - r[:, None]
    dc = c[None, :] - c[:, None]
    ok_r = (r[None, :] >= rs[:, None]) & (r[None, :] < rs[:, None] + NA_WIN_H)
    ok_c = (c[None, :] >= cs[:, None]) & (c[None, :] < cs[:, None] + NA_WIN_W)
    ri = np.clip(dr + NA_WIN_H - 1, 0, 2 * NA_WIN_H - 2)
    ci = np.clip(dc + NA_WIN_W - 1, 0, 2 * NA_WIN_W - 2)
    ok = ok_r[:, None, :, None] & ok_c[None, :, None, :]
    sel_r = jnp.asarray(ri[:, :, None] == np.arange(2 * NA_WIN_H - 1), F32)
    sel_c = jnp.asarray(ci[:, :, None] == np.arange(2 * NA_WIN_W - 1), F32)
    by_row = jnp.einsum('hij,qki->hqkj', rpb, sel_r, precision=lax.Precision.HIGHEST)
    tab = jnp.einsum('hqkj,cdj->hqckd', by_row, sel_c, precision=lax.Precision.HIGHEST)
    tab = jnp.where(ok[None], tab, NEG)
    return tab.reshape(NA_HEADS, rows * GRID_W, rows * GRID_W).astype(BF16)


def _proj_res_kernel(o_ref, w_ref, x_ref, mods_ref, out_ref, *, gate_row):
    y = jnp.dot(o_ref[...], w_ref[...], preferred_element_type=F32)
    out_ref[...] = x_ref[...] + mods_ref[gate_row:gate_row + 1, :] * y


def _proj_res(o, w, x, mods, gate_row, tm=512):
    B, S, K = o.shape
    tok = pl.BlockSpec((None, tm, D_MODEL), lambda b, i: (b, i, 0))
    return pl.pallas_call(
        functools.partial(_proj_res_kernel, gate_row=gate_row),
        out_shape=jax.ShapeDtypeStruct((B, S, D_MODEL), F32),
        grid=(B, S // tm),
        in_specs=[pl.BlockSpec((None, tm, K), lambda b, i: (b, i, 0)),
                  pl.BlockSpec(w.shape, lambda b, i: (0, 0)),
                  tok, pl.BlockSpec((None, 8, D_MODEL), lambda b, i: (b, 0, 0))],
        out_specs=tok,
        compiler_params=_params("parallel", "parallel"), name="proj_res",
    )(o, w, x, mods)


def _rope(y, cosf, sinf, width):
    lane = lax.broadcasted_iota(jnp.int32, (1, width), 1) % MLA_SLOT
    half = MLA_ROPE // 2
    up = pltpu.roll(y, width - half, 1)
    dn = pltpu.roll(y, half, 1)
    lo = (lane >= MLA_NOPE) & (lane < MLA_NOPE + half)
    hi = (lane >= MLA_NOPE + half) & (lane < MLA_QK)
    rot = jnp.where(lo, -up, jnp.where(hi, dn, 0.0))
    return y * cosf + rot * sinf


def _mla_proj_kernel(x_ref, mods_ref, g_ref, wd_ref, gcq_ref, gckv_ref, wuq_ref, wukv_ref,
                     pq_ref, pk_ref, gq_ref, gkn_ref, gkpe_ref, cos_ref, sin_ref,
                     q_ref, k_ref, v_ref):
    hn = _modnorm(x_ref[...], g_ref[...], mods_ref[1:2, :], mods_ref[0:1, :]).astype(BF16)
    down = jnp.dot(hn, wd_ref[...], preferred_element_type=F32)
    cq = _rmsnorm_rows(down[:, :MLA_Q_RANK], gcq_ref[...]).astype(BF16)
    ckv = _rmsnorm_rows(down[:, MLA_Q_RANK:MLA_Q_RANK + MLA_KV_RANK], gckv_ref[...]).astype(BF16)
    cosf = cos_ref[...]
    sinf = sin_ref[...]
    kpe = down[:, MLA_Q_RANK + MLA_KV_RANK:]
    kpe = _segnorm(kpe, pq_ref[:MLA_SLOT, :MLA_SLOT], gkpe_ref[...])
    kpe = _rope(kpe, cosf[:, :MLA_SLOT], sinf[:, :MLA_SLOT], MLA_SLOT)
    kpe2 = jnp.concatenate([kpe, kpe], axis=1)
    lane = lax.broadcasted_iota(jnp.int32, (1, 256), 1) % MLA_SLOT
    ones_lane = jnp.where(lane == MLA_V, 1.0, 0.0)
    n_tiles = MLA_HEADS * MLA_SLOT // 256
    for t in range(n_tiles):
        cols = slice(t * 256, (t + 1) * 256)
        y = jnp.dot(cq, wuq_ref[:, cols], preferred_element_type=F32)
        y = _segnorm(y, pq_ref[...], gq_ref[...])
        q_ref[:, cols] = _rope(y, cosf, sinf, 256).astype(BF16)
    for t in range(n_tiles):
        cols = slice(t * 256, (t + 1) * 256)
        y = jnp.dot(ckv, wukv_ref[:, cols], preferred_element_type=F32)
        y = _segnorm(y, pk_ref[...], gkn_ref[...])
        k_ref[:, cols] = (y + kpe2).astype(BF16)
    off = MLA_HEADS * MLA_SLOT
    for t in range(n_tiles):
        cols = slice(t * 256, (t + 1) * 256)
        y = jnp.dot(ckv, wukv_ref[:, off + t * 256:off + (t + 1) * 256], preferred_element_type=F32)
        v_ref[:, cols] = (y + ones_lane).astype(BF16)


def _mla_proj(x, mods, g, wd, gcq, gckv, wuq, wukv, pq, pk, gq, gkn, gkpe, cosf, sinf, tm=512):
    B, S, _ = x.shape
    wide = MLA_HEADS * MLA_SLOT
    tok = pl.BlockSpec((None, tm, D_MODEL), lambda b, i: (b, i, 0))
    full = lambda a: pl.BlockSpec(a.shape, lambda b, i: (0,) * a.ndim)
    pos = pl.BlockSpec((tm, 256), lambda b, i: (i, 0))
    out = jax.ShapeDtypeStruct((B, S, wide), BF16)
    ospec = pl.BlockSpec((None, tm, wide), lambda b, i: (b, i, 0))
    return pl.pallas_call(
        _mla_proj_kernel,
        out_shape=(out, out, out),
        grid=(B, S // tm),
        in_specs=[tok, pl.BlockSpec((None, 8, D_MODEL), lambda b, i: (b, 0, 0)),
                  full(g), full(wd), full(gcq), full(gckv), full(wuq), full(wukv),
                  full(pq), full(pk), full(gq), full(gkn), full(gkpe), pos, pos],
        out_specs=(ospec, ospec, ospec),
        compiler_params=_params("parallel", "parallel"), name="mla_proj",
    )(x, mods, g, wd, gcq, gckv, wuq, wukv, pq, pk, gq, gkn, gkpe, cosf, sinf)


def _flash_kernel(q_ref, k_ref, v_ref, o_ref, *, tk):
    q = q_ref[...]
    tq = q.shape[0]
    nk = k_ref.shape[0] // tk

    def body(j, carry):
        m, acc = carry
        ks = pl.multiple_of(j * tk, tk)
        k = k_ref[pl.ds(ks, tk), :]
        v = v_ref[pl.ds(ks, tk), :]
        s = lax.dot_general(q, k, NT_DIMS, preferred_element_type=F32)
        m_new = jnp.maximum(m, jnp.max(s, axis=-1, keepdims=True))
        alpha = jnp.exp2(m - m_new)
        p = jnp.exp2(s - m_new)
        acc = alpha * acc + jnp.dot(p.astype(BF16), v, preferred_element_type=F32)
        return m_new, acc

    m0 = jnp.full((tq, 1), -jnp.inf, F32)
    acc0 = jnp.zeros((tq, MLA_SLOT), F32)
    _, acc = lax.fori_loop(0, nk, body, (m0, acc0))
    o_ref[...] = (acc / acc[:, MLA_V:MLA_V + 1]).astype(BF16)


def _flash_bounded_kernel(q_ref, k_ref, v_ref, o_ref, acc_ref, *, tk, sub):
    tq = q_ref.shape[0]
    nk = k_ref.shape[0] // tk
    acc_ref[...] = jnp.zeros_like(acc_ref)

    def body(j, carry):
        ks = pl.multiple_of(j * tk, tk)
        k = k_ref[pl.ds(ks, tk), :]
        v = v_ref[pl.ds(ks, tk), :]
        for a in range(tq // sub):
            rows = slice(a * sub, (a + 1) * sub)
            s = lax.dot_general(q_ref[rows, :], k, NT_DIMS, preferred_element_type=F32)
            acc_ref[rows, :] += jnp.dot(jnp.exp2(s).astype(BF16), v, preferred_element_type=F32)
        return carry

    lax.fori_loop(0, nk, body, 0, unroll=2)
    acc = acc_ref[...]
    o_ref[...] = (acc / acc[:, MLA_V:MLA_V + 1]).astype(BF16)


def _flash_call(kernel_fn, q, k, v, tq, scratch, name):
    B, S, wide = q.shape
    H = wide // MLA_SLOT
    qspec = pl.BlockSpec((None, tq, MLA_SLOT), lambda b, h, i: (b, i, h))
    kvspec = pl.BlockSpec((None, S, MLA_SLOT), lambda b, h, i: (b, 0, h))
    return pl.pallas_call(
        kernel_fn,
        out_shape=jax.ShapeDtypeStruct((B, S, wide), BF16),
        grid=(B, H, S // tq),
        in_specs=[qspec, kvspec, kvspec],
        out_specs=qspec,
        scratch_shapes=scratch,
        compiler_params=_params("parallel", "parallel", "arbitrary"), name=name,
    )(q, k, v)


def _flash(q, k, v, bounded):
    fast = functools.partial(
        _flash_call, functools.partial(_flash_bounded_kernel, tk=1024, sub=256),
        tq=1024, scratch=[pltpu.VMEM((1024, MLA_SLOT), F32)], name="mla_flash_bounded")
    general = functools.partial(
        _flash_call, functools.partial(_flash_kernel, tk=512),
        tq=512, scratch=[], name="mla_flash")
    return lax.cond(bounded, fast, general, q, k, v)


def _mla_logit_bound(g_q, g_k):
    def seg(g):
        return MLA_NOPE * jnp.max(jnp.abs(g[:MLA_NOPE])) ** 2 + MLA_ROPE * jnp.max(jnp.abs(g[MLA_NOPE:])) ** 2
    return 1.05 * jnp.sqrt(seg(g_q) * seg(g_k)) * MLA_QK ** -0.5 * LOG2E


def _pack_halves(y):
    w = y.shape[1] // 2
    lo = pltpu.bitcast(y[:, :w].astype(BF16).astype(F32), jnp.int32)
    hi = pltpu.bitcast(y[:, w:].astype(BF16).astype(F32), jnp.int32)
    return (hi & HIGH_HALF) | lax.shift_right_logical(lo, 16)


def _unpack_halves(p):
    return pltpu.bitcast(p << 16, F32), pltpu.bitcast(p & HIGH_HALF, F32)


def _router_kernel(x_ref, mods_ref, g_ref, wrt_ref, br_ref, tri_ref,
                   hn_ref, eid_ref, wts_ref, rank_ref, cnt_ref, run_ref):
    @pl.when((pl.program_id(0) == 0) & (pl.program_id(1) == 0))
    def _():
        run_ref[...] = jnp.zeros_like(run_ref)

    hn = _modnorm(x_ref[...], g_ref[...], mods_ref[4:5, :], mods_ref[3:4, :])
    hn_ref[...] = _pack_halves(hn)
    tm = hn.shape[0]
    logits = lax.dot_general(wrt_ref[...], hn, NT_DIMS, preferred_element_type=F32,
                             precision=lax.Precision.HIGHEST)
    scores = _sigmoid(logits)
    choice = scores + br_ref[...]
    per = N_EXPERTS // N_GROUPS
    sub8 = lax.broadcasted_iota(jnp.int32, (per, tm), 0).astype(F32)
    gs = []
    for gi in range(N_GROUPS):
        cg = choice[gi * per:(gi + 1) * per, :]
        m1 = jnp.max(cg, axis=0, keepdims=True)
        i1 = jnp.min(jnp.where(cg == m1, sub8, float(per)), axis=0, keepdims=True)
        m2 = jnp.max(jnp.where(sub8 == i1, -jnp.inf, cg), axis=0, keepdims=True)
        gs.append(m1 + m2)
    gs8 = jnp.concatenate(gs, axis=0)
    rank = jnp.zeros_like(gs8)
    for gi in range(N_GROUPS):
        row = gs8[gi:gi + 1, :]
        ahead = (row > gs8) | ((row == gs8) & (sub8 > float(gi)))
        rank = rank + jnp.where(ahead, 1.0, 0.0)
    cur = jnp.concatenate(
        [jnp.where(rank[gi:gi + 1, :] < float(TOPK_GROUPS),
                   choice[gi * per:(gi + 1) * per, :], -jnp.inf)
         for gi in range(N_GROUPS)], axis=0)
    sub64 = lax.broadcasted_iota(jnp.int32, (N_EXPERTS, tm), 0).astype(F32)
    hits = []
    for _ in range(TOP_K):
        m = jnp.max(cur, axis=0, keepdims=True)
        idx = jnp.min(jnp.where(cur == m, sub64, float(N_EXPERTS)), axis=0, keepdims=True)
        hit = sub64 == idx
        hits.append((hit, idx))
        cur = jnp.where(hit, -jnp.inf, cur)
    chosen = jnp.zeros_like(cur)
    for hit, _ in hits:
        chosen = jnp.where(hit, 1.0, chosen)
    prefix = jnp.dot(chosen.astype(BF16), tri_ref[...], preferred_element_type=F32)
    rank_full = run_ref[:, 0:1] + prefix
    slot = lax.broadcasted_iota(jnp.int32, (8, tm), 0)
    eid8 = jnp.zeros((8, tm), F32)
    rank8 = jnp.zeros((8, tm), F32)
    w8 = jnp.zeros((8, tm), F32)
    for k, (hit, idx) in enumerate(hits):
        eid8 = jnp.where(slot == k, idx, eid8)
        rank8 = jnp.where(slot == k, jnp.sum(jnp.where(hit, rank_full, 0.0), axis=0, keepdims=True), rank8)
        w8 = jnp.where(slot == k, jnp.sum(jnp.where(hit, scores, 0.0), axis=0, keepdims=True), w8)
    eid_ref[...] = eid8.astype(jnp.int32)
    rank_ref[...] = rank8.astype(jnp.int32)
    wts_ref[...] = w8 / jnp.sum(w8, axis=0, keepdims=True) * ROUTED_SCALE
    run_ref[...] = run_ref[...] + jnp.sum(chosen, axis=1, keepdims=True)
    cnt_ref[...] = run_ref[...]


def _router(x, mods, g, wrt, br, tm=512):
    B, S, _ = x.shape
    tok = pl.BlockSpec((None, tm, D_MODEL), lambda b, i: (b, i, 0))
    slots = pl.BlockSpec((None, 8, tm), lambda b, i: (b, 0, i))
    tri = jnp.asarray(np.triu(np.ones((tm, tm), np.float32), 1), BF16)
    const = lambda a: pl.BlockSpec(a.shape, lambda b, i: (0, 0))
    return pl.pallas_call(
        _router_kernel,
        out_shape=(jax.ShapeDtypeStruct((B, S, D_MODEL // 2), jnp.int32),
                   jax.ShapeDtypeStruct((B, 8, S), jnp.int32),
                   jax.ShapeDtypeStruct((B, 8, S), F32),
                   jax.ShapeDtypeStruct((B, 8, S), jnp.int32),
                   jax.ShapeDtypeStruct((N_EXPERTS, 128), F32)),
        grid=(B, S // tm),
        in_specs=[tok, pl.BlockSpec((None, 8, D_MODEL), lambda b, i: (b, 0, 0)),
                  const(g), const(wrt), const(br), const(tri)],
        out_specs=(pl.BlockSpec((None, tm, D_MODEL // 2), lambda b, i: (b, i, 0)),
                   slots, slots, slots,
                   pl.BlockSpec((N_EXPERTS, 128), lambda b, i: (0, 0))),
        scratch_shapes=[pltpu.VMEM((N_EXPERTS, 128), F32)],
        compiler_params=_params("arbitrary", "arbitrary"), name="moe_router",
    )(x, mods, g, wrt, br, tri)


def _sc_gather(table, idx):
    R = idx.shape[0]
    W = table.shape[1]
    per_w = R // SC_WORKERS
    steps = per_w // SC_GATHER_ROWS
    assert per_w * SC_WORKERS == R and steps * SC_GATHER_ROWS == per_w and steps % 2 == 0
    idx3 = idx.reshape(SC_WORKERS, steps, SC_GATHER_ROWS)
    mesh = plsc.VectorSubcoreMesh(core_axis_name="c", subcore_axis_name="s")

    @functools.partial(
        pl.kernel, mesh=mesh,
        out_type=jax.ShapeDtypeStruct((R, W), jnp.int32),
        scratch_types=[pltpu.VMEM((steps, SC_GATHER_ROWS), jnp.int32),
                       pltpu.VMEM((2, SC_GATHER_ROWS, W), jnp.int32),
                       pltpu.SemaphoreType.DMA((2,)),
                       pltpu.SemaphoreType.DMA((2,))],
    )
    def gather_kernel(table_hbm, idx_hbm, out_hbm, idx_v, rows_v, gsem, wsem):
        wid = lax.axis_index("s") * SC_CORES + lax.axis_index("c")
        base = wid * per_w
        pltpu.sync_copy(idx_hbm.at[wid], idx_v)

        def gather(j, b):
            return pltpu.make_async_copy(table_hbm.at[idx_v.at[j]], rows_v.at[b], gsem.at[b])

        def write(j, b):
            return pltpu.make_async_copy(
                rows_v.at[b], out_hbm.at[pl.ds(base + j * SC_GATHER_ROWS, SC_GATHER_ROWS)], wsem.at[b])

        gather(0, 0).start()
        gather(1, 1).start()

        @pl.loop(0, steps, step=2)
        def _(j):
            for b in range(2):
                gather(j + b, b).wait()
                write(j + b, b).start()
                write(j + b, b).wait()

                @pl.when(j + b + 2 < steps)
                def _():
                    gather(j + b + 2, b).start()

    return gather_kernel(table, idx3)


def _dispatch_plan(eid, rank, counts, n_rows):
    B, _, S = eid.shape
    eid = eid[:, :TOP_K, :]
    rank = rank[:, :TOP_K, :]
    counts = counts[:, 0].astype(jnp.int32)
    padded = (counts + EXPERT_TILE - 1) // EXPERT_TILE * EXPERT_TILE
    ends = jnp.cumsum(padded)
    offs = ends - padded
    experts = jnp.arange(N_EXPERTS, dtype=jnp.int32)
    row = rank + jnp.sum(jnp.where(eid[..., None] == experts, offs, 0), axis=-1)
    tok = jnp.arange(B * S, dtype=jnp.int32).reshape(B, 1, S)
    src = jnp.zeros((n_rows,), jnp.int32).at[row.reshape(-1)].set(
        jnp.broadcast_to(tok, row.shape).reshape(-1), unique_indices=True)
    tile_start = jnp.arange(n_rows // EXPERT_TILE, dtype=jnp.int32) * EXPERT_TILE
    tile_expert = jnp.minimum(jnp.sum(tile_start[:, None] >= ends[None, :], axis=1), N_EXPERTS - 1)
    n_tiles = (ends[-1] // EXPERT_TILE).reshape(1)
    row_tokmajor = jnp.transpose(row, (0, 2, 1)).reshape(-1)
    return src, row_tokmajor, tile_expert.astype(jnp.int32), n_tiles.astype(jnp.int32)


def _experts_kernel(te_ref, nt_ref, xs_ref, wg_ref, wu_ref, wd_ref, y_ref):
    i = pl.program_id(0)

    @pl.when(i < nt_ref[0])
    def _():
        lo, hi = _unpack_halves(xs_ref[...])
        x = jnp.concatenate([lo.astype(BF16), hi.astype(BF16)], axis=1)
        a = jnp.dot(x, wg_ref[...], preferred_element_type=F32)
        u = jnp.dot(x, wu_ref[...], preferred_element_type=F32)
        h = ((a * _sigmoid(a)) * u).astype(BF16)
        y_ref[...] = _pack_halves(jnp.dot(h, wd_ref[...], preferred_element_type=F32))

    @pl.when(i >= nt_ref[0])
    def _():
        y_ref[...] = jnp.zeros_like(y_ref)


def _experts(xs, tile_expert, n_tiles, wg, wu, wd):
    n_rows, half = xs.shape
    rows = pl.BlockSpec((EXPERT_TILE, half), lambda i, te, nt: (i, 0))
    return pl.pallas_call(
        _experts_kernel,
        out_shape=jax.ShapeDtypeStruct((n_rows, half), jnp.int32),
        grid_spec=pltpu.PrefetchScalarGridSpec(
            num_scalar_prefetch=2,
            grid=(n_rows // EXPERT_TILE,),
            in_specs=[rows,
                      pl.BlockSpec((None, D_MODEL, EXPERT_FF), lambda i, te, nt: (te[i], 0, 0)),
                      pl.BlockSpec((None, D_MODEL, EXPERT_FF), lambda i, te, nt: (te[i], 0, 0)),
                      pl.BlockSpec((None, EXPERT_FF, D_MODEL), lambda i, te, nt: (te[i], 0, 0))],
            out_specs=rows),
        compiler_params=_params("arbitrary"), name="moe_experts",
    )(tile_expert, n_tiles, xs, wg, wu, wd)


def _combine_kernel(yg_ref, wts_ref, hn_ref, wsg_ref, wsu_ref, wsd_ref, x_ref, mods_ref, out_ref):
    lo, hi = _unpack_halves(hn_ref[...])
    hn = jnp.concatenate([lo.astype(BF16), hi.astype(BF16)], axis=1)
    a = jnp.dot(hn, wsg_ref[...], preferred_element_type=F32)
    u = jnp.dot(hn, wsu_ref[...], preferred_element_type=F32)
    shared = jnp.dot(((a * _sigmoid(a)) * u).astype(BF16), wsd_ref[...], preferred_element_type=F32)
    half = D_MODEL // 2
    acc_lo = shared[:, :half]
    acc_hi = shared[:, half:]
    wts = wts_ref[...]
    for k in range(TOP_K):
        lo, hi = _unpack_halves(yg_ref[:, k * half:(k + 1) * half])
        wk = wts[:, k:k + 1]
        acc_lo = acc_lo + wk * lo
        acc_hi = acc_hi + wk * hi
    g2 = mods_ref[5:6, :]
    out_ref[:, :half] = x_ref[:, :half] + g2[:, :half] * acc_lo
    out_ref[:, half:] = x_ref[:, half:] + g2[:, half:] * acc_hi


def _combine(yg, wts, hn, wsg, wsu, wsd, x, mods, tm=512):
    B, S, _ = x.shape
    half = D_MODEL // 2
    tok = pl.BlockSpec((None, tm, D_MODEL), lambda b, i: (b, i, 0))
    const = lambda a: pl.BlockSpec(a.shape, lambda b, i: (0,) * a.ndim)
    return pl.pallas_call(
        _combine_kernel,
        out_shape=jax.ShapeDtypeStruct((B, S, D_MODEL), F32),
        grid=(B, S // tm),
        in_specs=[pl.BlockSpec((None, tm, TOP_K * half), lambda b, i: (b, i, 0)),
                  pl.BlockSpec((None, tm, 8), lambda b, i: (b, i, 0)),
                  pl.BlockSpec((None, tm, half), lambda b, i: (b, i, 0)),
                  const(wsg), const(wsu), const(wsd),
                  tok, pl.BlockSpec((None, 8, D_MODEL), lambda b, i: (b, 0, 0))],
        out_specs=tok,
        compiler_params=_params("parallel", "parallel"), name="moe_combine",
    )(yg, wts, hn, wsg, wsu, wsd, x, mods)


def _block_diag_mean(segments, width):
    p = np.zeros((width, width), np.float32)
    pos = 0
    while pos < width:
        for length, live in segments:
            if live:
                p[pos:pos + length, pos:pos + length] = 1.0 / length
            pos += length
    return jnp.asarray(p, BF16)


def _slot_layout(w, n_heads, per_head, take, slot=MLA_SLOT):
    k = w.shape[0]
    wh = w.reshape(k, n_heads, per_head)[:, :, take]
    wh = jnp.pad(wh, ((0, 0), (0, 0), (0, slot - wh.shape[-1])))
    return wh.reshape(k, n_heads * slot)


def _slot_vector(pieces):
    v = jnp.concatenate(pieces)
    v = jnp.pad(v, (0, MLA_SLOT - v.shape[0]))
    return jnp.tile(v, 2).reshape(1, 2 * MLA_SLOT)


def _rope_tables(S):
    half = MLA_ROPE // 2
    inv = 1.0 / (ROPE_THETA ** (jnp.arange(half, dtype=F32) / half))
    ang = jnp.arange(S, dtype=F32)[:, None] * inv[None, :]
    cos, sin = jnp.cos(ang), jnp.sin(ang)
    ones = jnp.ones((S, MLA_NOPE), F32)
    pad = jnp.zeros((S, MLA_SLOT - MLA_QK), F32)
    cosf = jnp.concatenate([ones, cos, cos, pad], axis=1)
    sinf = jnp.concatenate([0.0 * ones, sin, sin, pad], axis=1)
    return jnp.tile(cosf, (1, 2)), jnp.tile(sinf, (1, 2))


def _prepare(g_norm1, g_norm2, na_w_qkv, na_g_q, na_g_k, na_rpb, na_w_o,
             mla_w_down, mla_g_cq, mla_g_ckv, mla_w_uq, mla_w_ukv, mla_g_q, mla_g_k, mla_w_o,
             w_router, b_router, w_gate, w_up, w_down, ws_gate, ws_up, ws_down):
    na = dict(
        g=g_norm1[0].reshape(1, D_MODEL),
        w=na_w_qkv[0].astype(BF16),
        p=_block_diag_mean([(NA_HEAD_DIM, True)], 256),
        gq=(jnp.tile(na_g_q[0], NA_HEADS) * NA_HEAD_DIM ** -0.5).reshape(1, D_MODEL),
        gk=jnp.tile(na_g_k[0], NA_HEADS).reshape(1, D_MODEL),
        bias=_na_bias_table(na_rpb[0]),
        wo=na_w_o[0].astype(BF16),
    )
    wdn = mla_w_down[0]
    kpe_cols = jnp.pad(wdn[:, MLA_Q_RANK + MLA_KV_RANK:],
                       ((0, 0), (MLA_NOPE, MLA_SLOT - MLA_QK)))
    nope = np.arange(MLA_NOPE)
    wuq = mla_w_uq[0]
    wukv = mla_w_ukv[0]
    wo = mla_w_o[0].reshape(MLA_HEADS, MLA_V, D_MODEL)
    wo = jnp.pad(wo, ((0, 0), (0, MLA_SLOT - MLA_V), (0, 0))).reshape(MLA_HEADS * MLA_SLOT, D_MODEL)
    gq = mla_g_q[0] * (MLA_QK ** -0.5 * LOG2E)
    gk = mla_g_k[0]
    zeros_nope = jnp.zeros((MLA_NOPE,), F32)
    mla = dict(
        g=g_norm1[1].reshape(1, D_MODEL),
        wd=jnp.concatenate([wdn[:, :MLA_Q_RANK + MLA_KV_RANK], kpe_cols], axis=1).astype(BF16),
        gcq=mla_g_cq[0].reshape(1, MLA_Q_RANK),
        gckv=mla_g_ckv[0].reshape(1, MLA_KV_RANK),
        wuq=_slot_layout(wuq, MLA_HEADS, MLA_QK, np.arange(MLA_QK)).astype(BF16),
        wukv=jnp.concatenate(
            [_slot_layout(wukv, MLA_HEADS, MLA_NOPE + MLA_V, nope),
             _slot_layout(wukv, MLA_HEADS, MLA_NOPE + MLA_V, MLA_NOPE + np.arange(MLA_V))],
            axis=1).astype(BF16),
        pq=_block_diag_mean([(MLA_NOPE, True), (MLA_ROPE, True), (MLA_SLOT - MLA_QK, False)], 256),
        pk=_block_diag_mean([(MLA_NOPE, True), (MLA_SLOT - MLA_NOPE, False)], 256),
        gq=_slot_vector([gq]),
        gkn=_slot_vector([gk[:MLA_NOPE]]),
        gkpe=_slot_vector([zeros_nope, gk[MLA_NOPE:]])[:, :MLA_SLOT],
        wo=wo.astype(BF16),
        bounded=_mla_logit_bound(mla_g_q[0], mla_g_k[0]) < FLASH_LOGIT_BOUND,
    )
    moe = []
    for i in range(w_router.shape[0]):
        moe.append(dict(
            g=g_norm2[i].reshape(1, D_MODEL),
            wrt=w_router[i].T,
            br=b_router[i].reshape(N_EXPERTS, 1),
            wg=w_gate[i].astype(BF16), wu=w_up[i].astype(BF16), wd=w_down[i].astype(BF16),
            wsg=ws_gate[i].astype(BF16), wsu=ws_up[i].astype(BF16), wsd=ws_down[i].astype(BF16),
        ))
    return na, mla, moe


def _moe_layer(x, mods, p):
    B, S, _ = x.shape
    half = D_MODEL // 2
    hn, eid, wts, rank, counts = _router(x, mods, p["g"], p["wrt"], p["br"])
    n_rows = B * S * TOP_K + N_EXPERTS * EXPERT_TILE
    src, row_tokmajor, tile_expert, n_tiles = _dispatch_plan(eid, rank, counts, n_rows)
    xs = _sc_gather(hn.reshape(B * S, half), src)
    ys = _experts(xs, tile_expert, n_tiles, p["wg"], p["wu"], p["wd"])
    yg = _sc_gather(ys, row_tokmajor).reshape(B, S, TOP_K * half)
    return _combine(yg, jnp.swapaxes(wts, 1, 2), hn, p["wsg"], p["wsu"], p["wsd"], x, mods)


def _trunk(x, mods0, mods1, na, mla, moe):
    S = x.shape[1]
    q, k, v = _na_qkv(x, mods0, na["g"], na["w"], na["p"], na["gq"], na["gk"])
    o = _na_attn(q, k, v, na["bias"])
    x = _proj_res(o, na["wo"], x, mods0, 2)
    x = _moe_layer(x, mods0, moe[0])
    cosf, sinf = _rope_tables(S)
    q, k, v = _mla_proj(x, mods1, mla["g"], mla["wd"], mla["gcq"], mla["gckv"], mla["wuq"],
                        mla["wukv"], mla["pq"], mla["pk"], mla["gq"], mla["gkn"], mla["gkpe"],
                        cosf, sinf)
    o = _flash(q, k, v, mla["bounded"])
    x = _proj_res(o, mla["wo"], x, mods1, 2)
    x = _moe_layer(x, mods1, moe[1])
    return x


def kernel(x_prompt, x_sample, c_prompt, c_sample, g_norm1, g_norm2, w_ada, b_ada, na_w_qkv, na_g_q, na_g_k, na_rpb, na_w_o, mla_w_down, mla_g_cq, mla_g_ckv, mla_w_uq, mla_w_ukv, mla_g_q, mla_g_k, mla_w_o, w_router, b_router, w_gate, w_up, w_down, ws_gate, ws_up, ws_down):
    na, mla, moe = _prepare(g_norm1, g_norm2, na_w_qkv, na_g_q, na_g_k, na_rpb, na_w_o,
                            mla_w_down, mla_g_cq, mla_g_ckv, mla_w_uq, mla_w_ukv, mla_g_q, mla_g_k,
                            mla_w_o, w_router, b_router, w_gate, w_up, w_down,
                            ws_gate, ws_up, ws_down)
    bp, bs = c_prompt.shape[0], c_sample.shape[0]
    c_all = jnp.concatenate([c_prompt, c_sample], axis=0)
    c_all = jnp.pad(c_all, ((0, (-c_all.shape[0]) % 8), (0, 0)))
    mods = []
    for i in range(w_ada.shape[0]):
        m = _mods(c_all, w_ada[i], b_ada[i]).reshape(c_all.shape[0], 6, D_MODEL)
        mods.append(jnp.pad(m, ((0, 0), (0, 2), (0, 0))))
    y_prompt = _trunk(x_prompt, mods[0][:bp], mods[1][:bp], na, mla, moe)
    y_sample = _trunk(x_sample, mods[0][bp:bp + bs], mods[1][bp:bp + bs], na, mla, moe)
    return (y_prompt, y_sample)
```

```python
import functools
import math

import jax
import jax.numpy as jnp
import numpy as np
from jax import lax
from jax.experimental import pallas as pl
from jax.experimental.pallas import tpu as pltpu
from jax.experimental.pallas import tpu_sc as plsc

F32 = jnp.float32
BF16 = jnp.bfloat16

D_MODEL = 1024
GRID_W = 64
NA_HEADS = 16
NA_HEAD_DIM = 64
NA_WIN_H = 8
NA_WIN_W = 16
NA_Q_ROWS = 4
NA_K_ROWS = 3 * NA_Q_ROWS
MLA_HEADS = 16
MLA_Q_RANK = 384
MLA_KV_RANK = 256
MLA_NOPE = 64
MLA_ROPE = 32
MLA_V = 64
MLA_QK = MLA_NOPE + MLA_ROPE
MLA_SLOT = 128
ROPE_THETA = 10000.0
N_EXPERTS = 64
TOP_K = 6
N_GROUPS = 8
TOPK_GROUPS = 4
EXPERT_FF = 256
ROUTED_SCALE = 2.5
EXPERT_TILE = 512
SC_CORES = 2
SC_SUBCORES = 16
SC_WORKERS = SC_CORES * SC_SUBCORES
SC_GATHER_ROWS = 64
HIGH_HALF = -65536
EPS = 1e-6
NEG = -1e30
LOG2E = math.log2(math.e)
FLASH_LOGIT_BOUND = 60.0

VMEM_LIMIT = 56 * 1024 * 1024
NT_DIMS = (((1,), (1,)), ((), ()))


def _params(*sem):
    return pltpu.CompilerParams(dimension_semantics=sem, vmem_limit_bytes=VMEM_LIMIT)


def _sigmoid(x):
    return 1.0 / (1.0 + jnp.exp(-x))


def _modnorm(x, g, sc, sh):
    ms = jnp.mean(x * x, axis=-1, keepdims=True)
    return (x * lax.rsqrt(ms + EPS) * g) * (1.0 + sc) + sh


def _rmsnorm_rows(x, g):
    ms = jnp.mean(x * x, axis=-1, keepdims=True)
    return x * lax.rsqrt(ms + EPS) * g


def _segnorm(y, p, g):
    ms = jnp.dot((y * y).astype(BF16), p, preferred_element_type=F32)
    return y * lax.rsqrt(ms + EPS) * g


def _mods_kernel(c_ref, w_ref, b_ref, o_ref):
    c = c_ref[...]
    a = c * _sigmoid(c)
    o_ref[...] = jnp.dot(a, w_ref[...], preferred_element_type=F32,
                         precision=lax.Precision.HIGHEST) + b_ref[...]


def _mods(c_all, w, b):
    rows = c_all.shape[0]
    n = w.shape[1]
    tn = 1024
    return pl.pallas_call(
        _mods_kernel,
        out_shape=jax.ShapeDtypeStruct((rows, n), F32),
        grid=(n // tn,),
        in_specs=[pl.BlockSpec((rows, D_MODEL), lambda j: (0, 0)),
                  pl.BlockSpec((D_MODEL, tn), lambda j: (0, j)),
                  pl.BlockSpec((1, tn), lambda j: (0, j))],
        out_specs=pl.BlockSpec((rows, tn), lambda j: (0, j)),
        compiler_params=_params("arbitrary"), name="adaln_mods",
    )(c_all, w, b.reshape(1, n))


def _na_qkv_kernel(x_ref, mods_ref, g_ref, w_ref, p_ref, gq_ref, gk_ref, q_ref, k_ref, v_ref):
    hn = _modnorm(x_ref[...], g_ref[...], mods_ref[1:2, :], mods_ref[0:1, :]).astype(BF16)
    for part, o_ref, gg in ((0, q_ref, gq_ref), (1, k_ref, gk_ref), (2, v_ref, None)):
        for t in range(D_MODEL // 256):
            lo = t * 256
            y = jnp.dot(hn, w_ref[:, part * D_MODEL + lo:part * D_MODEL + lo + 256],
                        preferred_element_type=F32)
            if gg is not None:
                y = _segnorm(y, p_ref[...], gg[:, lo:lo + 256])
            o_ref[:, lo:lo + 256] = y.astype(BF16)


def _na_qkv(x, mods, g, w, p, gq, gk, tm=512):
    B, S, _ = x.shape
    tok = pl.BlockSpec((None, tm, D_MODEL), lambda b, i: (b, i, 0))
    full = lambda a: pl.BlockSpec(a.shape, lambda b, i: (0,) * a.ndim)
    out = jax.ShapeDtypeStruct((B, S, D_MODEL), BF16)
    return pl.pallas_call(
        _na_qkv_kernel,
        out_shape=(out, out, out),
        grid=(B, S // tm),
        in_specs=[tok, pl.BlockSpec((None, 8, D_MODEL), lambda b, i: (b, 0, 0)),
                  full(g), full(w), full(p), full(gq), full(gk)],
        out_specs=(tok, tok, tok),
        compiler_params=_params("parallel", "parallel"), name="na_qkv",
    )(x, mods, g, w, p, gq, gk)


def _na_attn_kernel(q_ref, k0_ref, k1_ref, k2_ref, v0_ref, v1_ref, v2_ref, bias_ref, o_ref):
    lane = lax.broadcasted_iota(jnp.int32, (1, 128), 1)
    first = lane < NA_HEAD_DIM
    for hp in range(NA_HEADS // 2):
        cols = slice(hp * 128, (hp + 1) * 128)
        q = q_ref[:, cols]
        k = jnp.concatenate([k0_ref[:, cols], k1_ref[:, cols], k2_ref[:, cols]], axis=0)
        v = jnp.concatenate([v0_ref[:, cols], v1_ref[:, cols], v2_ref[:, cols]], axis=0)
        outs = []
        for sub in range(2):
            keep = first if sub == 0 else jnp.logical_not(first)
            qm = jnp.where(keep, q, jnp.zeros_like(q))
            s = lax.dot_general(qm, k, NT_DIMS, preferred_element_type=F32)
            s = s + bias_ref[2 * hp + sub].astype(F32)
            m = jnp.max(s, axis=-1, keepdims=True)
            e = jnp.exp(s - m)
            l = jnp.sum(e, axis=-1, keepdims=True)
            o = jnp.dot(e.astype(BF16), v, preferred_element_type=F32)
            outs.append(o / l)
        o_ref[:, cols] = jnp.where(first, outs[0], outs[1]).astype(BF16)


def _na_attn(q, k, v, bias):
    B, S, _ = q.shape
    tq = NA_Q_ROWS * GRID_W
    ng = S // tq
    assert ng >= 3

    def kv_spec(j):
        return pl.BlockSpec((None, tq, D_MODEL),
                            lambda b, g: (b, jnp.clip(g - 1, 0, ng - 3) + j, 0))

    def bias_map(b, g):
        return (0, jnp.where(g == 0, 0, jnp.where(g == ng - 1, 2, 1)), 0)

    qspec = pl.BlockSpec((None, tq, D_MODEL), lambda b, g: (b, g, 0))
    return pl.pallas_call(
        _na_attn_kernel,
        out_shape=jax.ShapeDtypeStruct((B, S, D_MODEL), BF16),
        grid=(B, ng),
        in_specs=[qspec, kv_spec(0), kv_spec(1), kv_spec(2), kv_spec(0), kv_spec(1), kv_spec(2),
                  pl.BlockSpec((NA_HEADS, tq, 3 * tq), bias_map)],
        out_specs=qspec,
        compiler_params=_params("parallel", "arbitrary"), name="na_attn",
    )(q, k, k, k, v, v, v, bias)


def _na_bias_table(rpb):
    rows = NA_K_ROWS
    r = np.arange(rows)
    c = np.arange(GRID_W)
    rs = np.clip(r - NA_WIN_H // 2, 0, rows - NA_WIN_H)
    cs = np.clip(c - NA_WIN_W // 2, 0, GRID_W - NA_WIN_W)
    dr = r[None, :] - r[:, None]
    dc = c[None, :] - c[:, None]
    ok_r = (r[None, :] >= rs[:, None]) & (r[None, :] < rs[:, None] + NA_WIN_H)
    ok_c = (c[None, :] >= cs[:, None]) & (c[None, :] < cs[:, None] + NA_WIN_W)
    ri = np.clip(dr + NA_WIN_H - 1, 0, 2 * NA_WIN_H - 2)
    ci = np.clip(dc + NA_WIN_W - 1, 0, 2 * NA_WIN_W - 2)
    ok = ok_r[:, None, :, None] & ok_c[None, :, None, :]
    sel_r = jnp.asarray(ri[:, :, None] == np.arange(2 * NA_WIN_H - 1), F32)
    sel_c = jnp.asarray(ci[:, :, None] == np.arange(2 * NA_WIN_W - 1), F32)
    by_row = jnp.einsum('hij,qki->hqkj', rpb, sel_r, precision=lax.Precision.HIGHEST)
    tab = jnp.einsum('hqkj,cdj->hqckd', by_row, sel_c, precision=lax.Precision.HIGHEST)
    tab = jnp.where(ok[None], tab, NEG)
    return tab.reshape(NA_HEADS, rows * GRID_W, rows * GRID_W).astype(BF16)


def _proj_res_kernel(o_ref, w_ref, x_ref, mods_ref, out_ref, *, gate_row):
    y = jnp.dot(o_ref[...], w_ref[...], preferred_element_type=F32)
    out_ref[...] = x_ref[...] + mods_ref[gate_row:gate_row + 1, :] * y


def _proj_res(o, w, x, mods, gate_row, tm=512):
    B, S, K = o.shape
    tok = pl.BlockSpec((None, tm, D_MODEL), lambda b, i: (b, i, 0))
    return pl.pallas_call(
        functools.partial(_proj_res_kernel, gate_row=gate_row),
        out_shape=jax.ShapeDtypeStruct((B, S, D_MODEL), F32),
        grid=(B, S // tm),
        in_specs=[pl.BlockSpec((None, tm, K), lambda b, i: (b, i, 0)),
                  pl.BlockSpec(w.shape, lambda b, i: (0, 0)),
                  tok, pl.BlockSpec((None, 8, D_MODEL), lambda b, i: (b, 0, 0))],
        out_specs=tok,
        compiler_params=_params("parallel", "parallel"), name="proj_res",
    )(o, w, x, mods)


def _rope(y, cosf, sinf, width):
    lane = lax.broadcasted_iota(jnp.int32, (1, width), 1) % MLA_SLOT
    half = MLA_ROPE // 2
    up = pltpu.roll(y, width - half, 1)
    dn = pltpu.roll(y, half, 1)
    lo = (lane >= MLA_NOPE) & (lane < MLA_NOPE + half)
    hi = (lane >= MLA_NOPE + half) & (lane < MLA_QK)
    rot = jnp.where(lo, -up, jnp.where(hi, dn, 0.0))
    return y * cosf + rot * sinf


def _mla_proj_kernel(x_ref, mods_ref, g_ref, wd_ref, gcq_ref, gckv_ref, wuq_ref, wukv_ref, wvt_ref,
                     pq_ref, pk_ref, gq_ref, gkn_ref, gkpe_ref, cos_ref, sin_ref,
                     q_ref, k_ref, vt_ref):
    hn = _modnorm(x_ref[...], g_ref[...], mods_ref[1:2, :], mods_ref[0:1, :]).astype(BF16)
    down = jnp.dot(hn, wd_ref[...], preferred_element_type=F32)
    cq = _rmsnorm_rows(down[:, :MLA_Q_RANK], gcq_ref[...]).astype(BF16)
    ckv = _rmsnorm_rows(down[:, MLA_Q_RANK:MLA_Q_RANK + MLA_KV_RANK], gckv_ref[...]).astype(BF16)
    cosf = cos_ref[...]
    sinf = sin_ref[...]
    kpe = down[:, MLA_Q_RANK + MLA_KV_RANK:]
    kpe = _segnorm(kpe, pq_ref[:MLA_SLOT, :MLA_SLOT], gkpe_ref[...])
    kpe = _rope(kpe, cosf[:, :MLA_SLOT], sinf[:, :MLA_SLOT], MLA_SLOT)
    kpe2 = jnp.concatenate([kpe, kpe], axis=1)
    sub = lax.broadcasted_iota(jnp.int32, (256, 1), 0) % MLA_SLOT
    ones_row = jnp.where(sub == MLA_V, 1.0, 0.0)
    n_tiles = MLA_HEADS * MLA_SLOT // 256
    for t in range(n_tiles):
        cols = slice(t * 256, (t + 1) * 256)
        y = jnp.dot(cq, wuq_ref[:, cols], preferred_element_type=F32)
        y = _segnorm(y, pq_ref[...], gq_ref[...])
        q_ref[:, cols] = _rope(y, cosf, sinf, 256).astype(BF16)
    for t in range(n_tiles):
        cols = slice(t * 256, (t + 1) * 256)
        y = jnp.dot(ckv, wukv_ref[:, cols], preferred_element_type=F32)
        y = _segnorm(y, pk_ref[...], gkn_ref[...])
        k_ref[:, cols] = (y + kpe2).astype(BF16)
    for t in range(n_tiles):
        rows = slice(t * 256, (t + 1) * 256)
        yt = lax.dot_general(wvt_ref[rows, :], ckv, NT_DIMS, preferred_element_type=F32)
        vt_ref[rows, :] = (yt + ones_row).astype(BF16)


def _mla_proj(x, mods, g, wd, gcq, gckv, wuq, wukv, wvt, pq, pk, gq, gkn, gkpe, cosf, sinf, tm=512):
    B, S, _ = x.shape
    wide = MLA_HEADS * MLA_SLOT
    tok = pl.BlockSpec((None, tm, D_MODEL), lambda b, i: (b, i, 0))
    full = lambda a: pl.BlockSpec(a.shape, lambda b, i: (0,) * a.ndim)
    pos = pl.BlockSpec((tm, 256), lambda b, i: (i, 0))
    out = jax.ShapeDtypeStruct((B, S, wide), BF16)
    ospec = pl.BlockSpec((None, tm, wide), lambda b, i: (b, i, 0))
    return pl.pallas_call(
        _mla_proj_kernel,
        out_shape=(out, out, jax.ShapeDtypeStruct((B, wide, S), BF16)),
        grid=(B, S // tm),
        in_specs=[tok, pl.BlockSpec((None, 8, D_MODEL), lambda b, i: (b, 0, 0)),
                  full(g), full(wd), full(gcq), full(gckv), full(wuq), full(wukv), full(wvt),
                  full(pq), full(pk), full(gq), full(gkn), full(gkpe), pos, pos],
        out_specs=(ospec, ospec, pl.BlockSpec((None, wide, tm), lambda b, i: (b, 0, i))),
        compiler_params=_params("parallel", "parallel"), name="mla_proj",
    )(x, mods, g, wd, gcq, gckv, wuq, wukv, wvt, pq, pk, gq, gkn, gkpe, cosf, sinf)


def _flash_kernel(q_ref, k_ref, vt_ref, o_ref, *, tk):
    q = q_ref[...]
    tq = q.shape[0]
    nk = k_ref.shape[0] // tk

    def body(j, carry):
        m, acc = carry
        ks = pl.multiple_of(j * tk, tk)
        k = k_ref[pl.ds(ks, tk), :]
        vt = vt_ref[:, pl.ds(ks, tk)]
        s = lax.dot_general(q, k, NT_DIMS, preferred_element_type=F32)
        m_new = jnp.maximum(m, jnp.max(s, axis=-1, keepdims=True))
        alpha = jnp.exp2(m - m_new)
        p = jnp.exp2(s - m_new)
        acc = alpha * acc + lax.dot_general(p.astype(BF16), vt, NT_DIMS, preferred_element_type=F32)
        return m_new, acc

    m0 = jnp.full((tq, 1), -jnp.inf, F32)
    acc0 = jnp.zeros((tq, MLA_SLOT), F32)
    _, acc = lax.fori_loop(0, nk, body, (m0, acc0))
    o_ref[...] = (acc / acc[:, MLA_V:MLA_V + 1]).astype(BF16)


def _flash_bounded_kernel(q_ref, k_ref, vt_ref, o_ref, acc_ref, *, tk, sub):
    nk = k_ref.shape[0] // tk
    acc_ref[...] = jnp.zeros_like(acc_ref)
    q = q_ref[...]

    def body(j, carry):
        for a in range(tk // sub):
            ks = pl.multiple_of(j * tk + a * sub, sub)
            st = lax.dot_general(k_ref[pl.ds(ks, sub), :], q, NT_DIMS, preferred_element_type=F32)
            acc_ref[...] += jnp.dot(vt_ref[:, pl.ds(ks, sub)], jnp.exp2(st).astype(BF16),
                                    preferred_element_type=F32)
        return carry

    lax.fori_loop(0, nk, body, 0, unroll=2)
    acc = acc_ref[...]
    o_ref[...] = jnp.transpose(acc / acc[MLA_V:MLA_V + 1, :]).astype(BF16)


def _flash_call(kernel_fn, q, k, vt, tq, scratch, name):
    B, S, wide = q.shape
    H = wide // MLA_SLOT
    qspec = pl.BlockSpec((None, tq, MLA_SLOT), lambda b, h, i: (b, i, h))
    kspec = pl.BlockSpec((None, S, MLA_SLOT), lambda b, h, i: (b, 0, h))
    vtspec = pl.BlockSpec((None, MLA_SLOT, S), lambda b, h, i: (b, h, 0))
    return pl.pallas_call(
        kernel_fn,
        out_shape=jax.ShapeDtypeStruct((B, S, wide), BF16),
        grid=(B, H, S // tq),
        in_specs=[qspec, kspec, vtspec],
        out_specs=qspec,
        scratch_shapes=scratch,
        compiler_params=_params("parallel", "parallel", "arbitrary"), name=name,
    )(q, k, vt)


def _flash(q, k, vt, bounded):
    fast = functools.partial(
        _flash_call, functools.partial(_flash_bounded_kernel, tk=1024, sub=512),
        tq=2048, scratch=[pltpu.VMEM((MLA_SLOT, 2048), F32)], name="mla_flash_bounded")
    general = functools.partial(
        _flash_call, functools.partial(_flash_kernel, tk=512),
        tq=512, scratch=[], name="mla_flash")
    return lax.cond(bounded, fast, general, q, k, vt)


def _mla_logit_bound(g_q, g_k):
    def seg(g):
        return MLA_NOPE * jnp.max(jnp.abs(g[:MLA_NOPE])) ** 2 + MLA_ROPE * jnp.max(jnp.abs(g[MLA_NOPE:])) ** 2
    return 1.05 * jnp.sqrt(seg(g_q) * seg(g_k)) * MLA_QK ** -0.5 * LOG2E


def _pack_halves(y):
    w = y.shape[1] // 2
    lo = pltpu.bitcast(y[:, :w].astype(BF16).astype(F32), jnp.int32)
    hi = pltpu.bitcast(y[:, w:].astype(BF16).astype(F32), jnp.int32)
    return (hi & HIGH_HALF) | lax.shift_right_logical(lo, 16)


def _unpack_halves(p):
    return pltpu.bitcast(p << 16, F32), pltpu.bitcast(p & HIGH_HALF, F32)


def _router_kernel(x_ref, mods_ref, g_ref, wrt_ref, br_ref, tri_ref,
                   hn_ref, eid_ref, wts_ref, rank_ref, cnt_ref, run_ref):
    @pl.when((pl.program_id(0) == 0) & (pl.program_id(1) == 0))
    def _():
        run_ref[...] = jnp.zeros_like(run_ref)

    hn = _modnorm(x_ref[...], g_ref[...], mods_ref[4:5, :], mods_ref[3:4, :])
    hn_ref[...] = _pack_halves(hn)
    tm = hn.shape[0]
    logits = lax.dot_general(wrt_ref[...], hn, NT_DIMS, preferred_element_type=F32,
                             precision=lax.Precision.HIGHEST)
    scores = _sigmoid(logits)
    choice = scores + br_ref[...]
    per = N_EXPERTS // N_GROUPS
    sub8 = lax.broadcasted_iota(jnp.int32, (per, tm), 0).astype(F32)
    gs = []
    for gi in range(N_GROUPS):
        cg = choice[gi * per:(gi + 1) * per, :]
        m1 = jnp.max(cg, axis=0, keepdims=True)
        i1 = jnp.min(jnp.where(cg == m1, sub8, float(per)), axis=0, keepdims=True)
        m2 = jnp.max(jnp.where(sub8 == i1, -jnp.inf, cg), axis=0, keepdims=True)
        gs.append(m1 + m2)
    gs8 = jnp.concatenate(gs, axis=0)
    rank = jnp.zeros_like(gs8)
    for gi in range(N_GROUPS):
        row = gs8[gi:gi + 1, :]
        ahead = (row > gs8) | ((row == gs8) & (sub8 > float(gi)))
        rank = rank + jnp.where(ahead, 1.0, 0.0)
    cur = jnp.concatenate(
        [jnp.where(rank[gi:gi + 1, :] < float(TOPK_GROUPS),
                   choice[gi * per:(gi + 1) * per, :], -jnp.inf)
         for gi in range(N_GROUPS)], axis=0)
    sub64 = lax.broadcasted_iota(jnp.int32, (N_EXPERTS, tm), 0).astype(F32)
    hits = []
    for _ in range(TOP_K):
        m = jnp.max(cur, axis=0, keepdims=True)
        idx = jnp.min(jnp.where(cur == m, sub64, float(N_EXPERTS)), axis=0, keepdims=True)
        hit = sub64 == idx
        hits.append((hit, idx))
        cur = jnp.where(hit, -jnp.inf, cur)
    chosen = jnp.zeros_like(cur)
    for hit, _ in hits:
        chosen = jnp.where(hit, 1.0, chosen)
    prefix = jnp.dot(chosen.astype(BF16), tri_ref[...], preferred_element_type=F32)
    rank_full = run_ref[:, 0:1] + prefix
    slot = lax.broadcasted_iota(jnp.int32, (8, tm), 0)
    eid8 = jnp.zeros((8, tm), F32)
    rank8 = jnp.zeros((8, tm), F32)
    w8 = jnp.zeros((8, tm), F32)
    for k, (hit, idx) in enumerate(hits):
        eid8 = jnp.where(slot == k, idx, eid8)
        rank8 = jnp.where(slot == k, jnp.sum(jnp.where(hit, rank_full, 0.0), axis=0, keepdims=True), rank8)
        w8 = jnp.where(slot == k, jnp.sum(jnp.where(hit, scores, 0.0), axis=0, keepdims=True), w8)
    eid_ref[...] = eid8.astype(jnp.int32)
    rank_ref[...] = rank8.astype(jnp.int32)
    wts_ref[...] = w8 / jnp.sum(w8, axis=0, keepdims=True) * ROUTED_SCALE
    run_ref[...] = run_ref[...] + jnp.sum(chosen, axis=1, keepdims=True)
    cnt_ref[...] = run_ref[...]


def _router(x, mods, g, wrt, br, tm=512):
    B, S, _ = x.shape
    tok = pl.BlockSpec((None, tm, D_MODEL), lambda b, i: (b, i, 0))
    slots = pl.BlockSpec((None, 8, tm), lambda b, i: (b, 0, i))
    tri = jnp.asarray(np.triu(np.ones((tm, tm), np.float32), 1), BF16)
    const = lambda a: pl.BlockSpec(a.shape, lambda b, i: (0, 0))
    return pl.pallas_call(
        _router_kernel,
        out_shape=(jax.ShapeDtypeStruct((B, S, D_MODEL // 2), jnp.int32),
                   jax.ShapeDtypeStruct((B, 8, S), jnp.int32),
                   jax.ShapeDtypeStruct((B, 8, S), F32),
                   jax.ShapeDtypeStruct((B, 8, S), jnp.int32),
                   jax.ShapeDtypeStruct((N_EXPERTS, 128), F32)),
        grid=(B, S // tm),
        in_specs=[tok, pl.BlockSpec((None, 8, D_MODEL), lambda b, i: (b, 0, 0)),
                  const(g), const(wrt), const(br), const(tri)],
        out_specs=(pl.BlockSpec((None, tm, D_MODEL // 2), lambda b, i: (b, i, 0)),
                   slots, slots, slots,
                   pl.BlockSpec((N_EXPERTS, 128), lambda b, i: (0, 0))),
        scratch_shapes=[pltpu.VMEM((N_EXPERTS, 128), F32)],
        compiler_params=_params("arbitrary", "arbitrary"), name="moe_router",
    )(x, mods, g, wrt, br, tri)


def _sc_gather(table, idx):
    R = idx.shape[0]
    W = table.shape[1]
    per_w = R // SC_WORKERS
    steps = per_w // SC_GATHER_ROWS
    assert per_w * SC_WORKERS == R and steps * SC_GATHER_ROWS == per_w and steps % 2 == 0
    idx3 = idx.reshape(SC_WORKERS, steps, SC_GATHER_ROWS)
    mesh = plsc.VectorSubcoreMesh(core_axis_name="c", subcore_axis_name="s")

    @functools.partial(
        pl.kernel, mesh=mesh,
        out_type=jax.ShapeDtypeStruct((R, W), jnp.int32),
        scratch_types=[pltpu.VMEM((steps, SC_GATHER_ROWS), jnp.int32),
                       pltpu.VMEM((2, SC_GATHER_ROWS, W), jnp.int32),
                       pltpu.SemaphoreType.DMA((2,)),
                       pltpu.SemaphoreType.DMA((2,))],
    )
    def gather_kernel(table_hbm, idx_hbm, out_hbm, idx_v, rows_v, gsem, wsem):
        wid = lax.axis_index("s") * SC_CORES + lax.axis_index("c")
        base = wid * per_w
        pltpu.sync_copy(idx_hbm.at[wid], idx_v)

        def gather(j, b):
            return pltpu.make_async_copy(table_hbm.at[idx_v.at[j]], rows_v.at[b], gsem.at[b])

        def write(j, b):
            return pltpu.make_async_copy(
                rows_v.at[b], out_hbm.at[pl.ds(base + j * SC_GATHER_ROWS, SC_GATHER_ROWS)], wsem.at[b])

        gather(0, 0).start()
        gather(1, 1).start()

        @pl.loop(0, steps, step=2)
        def _(j):
            for b in range(2):
                gather(j + b, b).wait()
                write(j + b, b).start()
                write(j + b, b).wait()

                @pl.when(j + b + 2 < steps)
                def _():
                    gather(j + b + 2, b).start()

    return gather_kernel(table, idx3)


def _sc_scatter(rows, idx, n_out):
    A = idx.shape[0]
    V, W = rows.shape
    per_w = A // SC_WORKERS
    steps = per_w // SC_GATHER_ROWS
    assert per_w * SC_WORKERS == A and steps * SC_GATHER_ROWS == per_w and steps % 2 == 0
    assert V % SC_GATHER_ROWS == 0
    idx3 = idx.reshape(SC_WORKERS, steps, SC_GATHER_ROWS)
    mesh = plsc.VectorSubcoreMesh(core_axis_name="c", subcore_axis_name="s")

    @functools.partial(
        pl.kernel, mesh=mesh,
        out_type=jax.ShapeDtypeStruct((n_out, W), jnp.int32),
        scratch_types=[pltpu.VMEM((steps, SC_GATHER_ROWS), jnp.int32),
                       pltpu.VMEM((2, SC_GATHER_ROWS, W), jnp.int32),
                       pltpu.SemaphoreType.DMA((2,)),
                       pltpu.SemaphoreType.DMA((2,))],
    )
    def scatter_kernel(rows_hbm, idx_hbm, out_hbm, idx_v, buf_v, rsem, wsem):
        wid = lax.axis_index("s") * SC_CORES + lax.axis_index("c")
        base = wid * per_w
        pltpu.sync_copy(idx_hbm.at[wid], idx_v)

        def read(j, b):
            src = lax.rem(base + j * SC_GATHER_ROWS, V)
            return pltpu.make_async_copy(rows_hbm.at[pl.ds(src, SC_GATHER_ROWS)], buf_v.at[b], rsem.at[b])

        def write(j, b):
            return pltpu.make_async_copy(buf_v.at[b], out_hbm.at[idx_v.at[j]], wsem.at[b])

        read(0, 0).start()
        read(1, 1).start()

        @pl.loop(0, steps, step=2)
        def _(j):
            for b in range(2):
                read(j + b, b).wait()
                write(j + b, b).start()
                write(j + b, b).wait()

                @pl.when(j + b + 2 < steps)
                def _():
                    read(j + b + 2, b).start()

    return scatter_kernel(rows, idx3)


def _dispatch_plan(eid, rank, counts, n_rows):
    eid = eid[:, :TOP_K, :]
    rank = rank[:, :TOP_K, :]
    counts = counts[:, 0].astype(jnp.int32)
    padded = (counts + EXPERT_TILE - 1) // EXPERT_TILE * EXPERT_TILE
    ends = jnp.cumsum(padded)
    offs = ends - padded
    experts = jnp.arange(N_EXPERTS, dtype=jnp.int32)
    row = rank + jnp.sum(jnp.where(eid[..., None] == experts, offs, 0), axis=-1)
    row_slotmajor = jnp.transpose(row, (1, 0, 2)).reshape(-1)
    tile_start = jnp.arange(n_rows // EXPERT_TILE, dtype=jnp.int32) * EXPERT_TILE
    tile_expert = jnp.minimum(jnp.sum(tile_start[:, None] >= ends[None, :], axis=1), N_EXPERTS - 1)
    live_end = jnp.sum(jnp.where(tile_expert[:, None] == experts, offs + counts, 0), axis=1)
    tile_live = jnp.clip(live_end - tile_start, 0, EXPERT_TILE)
    n_tiles = (ends[-1] // EXPERT_TILE).reshape(1)
    return (row_slotmajor, tile_expert.astype(jnp.int32), tile_live.astype(jnp.int32),
            n_tiles.astype(jnp.int32))


def _experts_kernel(te_ref, tl_ref, nt_ref, xs_ref, wg_ref, wu_ref, wd_ref, y_ref):
    i = pl.program_id(0)

    @pl.when(i < nt_ref[0])
    def _():
        live = lax.broadcasted_iota(jnp.int32, (EXPERT_TILE, 1), 0) < tl_ref[i]
        lo, hi = _unpack_halves(jnp.where(live, xs_ref[...], 0))
        x = jnp.concatenate([lo.astype(BF16), hi.astype(BF16)], axis=1)
        a = jnp.dot(x, wg_ref[...], preferred_element_type=F32)
        u = jnp.dot(x, wu_ref[...], preferred_element_type=F32)
        h = ((a * _sigmoid(a)) * u).astype(BF16)
        y_ref[...] = _pack_halves(jnp.dot(h, wd_ref[...], preferred_element_type=F32))

    @pl.when(i >= nt_ref[0])
    def _():
        y_ref[...] = jnp.zeros_like(y_ref)


def _experts(xs, tile_expert, tile_live, n_tiles, wg, wu, wd):
    n_rows, half = xs.shape
    rows = pl.BlockSpec((EXPERT_TILE, half), lambda i, te, tl, nt: (i, 0))
    return pl.pallas_call(
        _experts_kernel,
        out_shape=jax.ShapeDtypeStruct((n_rows, half), jnp.int32),
        grid_spec=pltpu.PrefetchScalarGridSpec(
            num_scalar_prefetch=3,
            grid=(n_rows // EXPERT_TILE,),
            in_specs=[rows,
                      pl.BlockSpec((None, D_MODEL, EXPERT_FF), lambda i, te, tl, nt: (te[i], 0, 0)),
                      pl.BlockSpec((None, D_MODEL, EXPERT_FF), lambda i, te, tl, nt: (te[i], 0, 0)),
                      pl.BlockSpec((None, EXPERT_FF, D_MODEL), lambda i, te, tl, nt: (te[i], 0, 0))],
            out_specs=rows),
        compiler_params=_params("arbitrary"), name="moe_experts",
    )(tile_expert, tile_live, n_tiles, xs, wg, wu, wd)


def _combine_kernel(yg_ref, wts_ref, hn_ref, wsg_ref, wsu_ref, wsd_ref, x_ref, mods_ref, out_ref):
    lo, hi = _unpack_halves(hn_ref[...])
    hn = jnp.concatenate([lo.astype(BF16), hi.astype(BF16)], axis=1)
    a = jnp.dot(hn, wsg_ref[...], preferred_element_type=F32)
    u = jnp.dot(hn, wsu_ref[...], preferred_element_type=F32)
    shared = jnp.dot(((a * _sigmoid(a)) * u).astype(BF16), wsd_ref[...], preferred_element_type=F32)
    half = D_MODEL // 2
    acc_lo = shared[:, :half]
    acc_hi = shared[:, half:]
    wts = wts_ref[...]
    for k in range(TOP_K):
        lo, hi = _unpack_halves(yg_ref[k])
        wk = wts[:, k:k + 1]
        acc_lo = acc_lo + wk * lo
        acc_hi = acc_hi + wk * hi
    g2 = mods_ref[5:6, :]
    out_ref[:, :half] = x_ref[:, :half] + g2[:, :half] * acc_lo
    out_ref[:, half:] = x_ref[:, half:] + g2[:, half:] * acc_hi


def _combine(yg, wts, hn, wsg, wsu, wsd, x, mods, tm=512):
    B, S, _ = x.shape
    half = D_MODEL // 2
    tok = pl.BlockSpec((None, tm, D_MODEL), lambda b, i: (b, i, 0))
    const = lambda a: pl.BlockSpec(a.shape, lambda b, i: (0,) * a.ndim)
    return pl.pallas_call(
        _combine_kernel,
        out_shape=jax.ShapeDtypeStruct((B, S, D_MODEL), F32),
        grid=(B, S // tm),
        in_specs=[pl.BlockSpec((TOP_K, None, tm, half), lambda b, i: (0, b, i, 0)),
                  pl.BlockSpec((None, tm, 8), lambda b, i: (b, i, 0)),
                  pl.BlockSpec((None, tm, half), lambda b, i: (b, i, 0)),
                  const(wsg), const(wsu), const(wsd),
                  tok, pl.BlockSpec((None, 8, D_MODEL), lambda b, i: (b, 0, 0))],
        out_specs=tok,
        compiler_params=_params("parallel", "parallel"), name="moe_combine",
    )(yg, wts, hn, wsg, wsu, wsd, x, mods)


def _block_diag_mean(segments, width):
    p = np.zeros((width, width), np.float32)
    pos = 0
    while pos < width:
        for length, live in segments:
            if live:
                p[pos:pos + length, pos:pos + length] = 1.0 / length
            pos += length
    return jnp.asarray(p, BF16)


def _slot_layout(w, n_heads, per_head, take, slot=MLA_SLOT):
    k = w.shape[0]
    wh = w.reshape(k, n_heads, per_head)[:, :, take]
    wh = jnp.pad(wh, ((0, 0), (0, 0), (0, slot - wh.shape[-1])))
    return wh.reshape(k, n_heads * slot)


def _slot_vector(pieces):
    v = jnp.concatenate(pieces)
    v = jnp.pad(v, (0, MLA_SLOT - v.shape[0]))
    return jnp.tile(v, 2).reshape(1, 2 * MLA_SLOT)


def _rope_tables(S):
    half = MLA_ROPE // 2
    inv = 1.0 / (ROPE_THETA ** (jnp.arange(half, dtype=F32) / half))
    ang = jnp.arange(S, dtype=F32)[:, None] * inv[None, :]
    cos, sin = jnp.cos(ang), jnp.sin(ang)
    ones = jnp.ones((S, MLA_NOPE), F32)
    pad = jnp.zeros((S, MLA_SLOT - MLA_QK), F32)
    cosf = jnp.concatenate([ones, cos, cos, pad], axis=1)
    sinf = jnp.concatenate([0.0 * ones, sin, sin, pad], axis=1)
    return jnp.tile(cosf, (1, 2)), jnp.tile(sinf, (1, 2))


def _prepare(g_norm1, g_norm2, na_w_qkv, na_g_q, na_g_k, na_rpb, na_w_o,
             mla_w_down, mla_g_cq, mla_g_ckv, mla_w_uq, mla_w_ukv, mla_g_q, mla_g_k, mla_w_o,
             w_router, b_router, w_gate, w_up, w_down, ws_gate, ws_up, ws_down):
    na = dict(
        g=g_norm1[0].reshape(1, D_MODEL),
        w=na_w_qkv[0].astype(BF16),
        p=_block_diag_mean([(NA_HEAD_DIM, True)], 256),
        gq=(jnp.tile(na_g_q[0], NA_HEADS) * NA_HEAD_DIM ** -0.5).reshape(1, D_MODEL),
        gk=jnp.tile(na_g_k[0], NA_HEADS).reshape(1, D_MODEL),
        bias=_na_bias_table(na_rpb[0]),
        wo=na_w_o[0].astype(BF16),
    )
    wdn = mla_w_down[0]
    kpe_cols = jnp.pad(wdn[:, MLA_Q_RANK + MLA_KV_RANK:],
                       ((0, 0), (MLA_NOPE, MLA_SLOT - MLA_QK)))
    nope = np.arange(MLA_NOPE)
    wuq = mla_w_uq[0]
    wukv = mla_w_ukv[0]
    wo = mla_w_o[0].reshape(MLA_HEADS, MLA_V, D_MODEL)
    wo = jnp.pad(wo, ((0, 0), (0, MLA_SLOT - MLA_V), (0, 0))).reshape(MLA_HEADS * MLA_SLOT, D_MODEL)
    gq = mla_g_q[0] * (MLA_QK ** -0.5 * LOG2E)
    gk = mla_g_k[0]
    zeros_nope = jnp.zeros((MLA_NOPE,), F32)
    mla = dict(
        g=g_norm1[1].reshape(1, D_MODEL),
        wd=jnp.concatenate([wdn[:, :MLA_Q_RANK + MLA_KV_RANK], kpe_cols], axis=1).astype(BF16),
        gcq=mla_g_cq[0].reshape(1, MLA_Q_RANK),
        gckv=mla_g_ckv[0].reshape(1, MLA_KV_RANK),
        wuq=_slot_layout(wuq, MLA_HEADS, MLA_QK, np.arange(MLA_QK)).astype(BF16),
        wukv=_slot_layout(wukv, MLA_HEADS, MLA_NOPE + MLA_V, nope).astype(BF16),
        wvt=_slot_layout(wukv, MLA_HEADS, MLA_NOPE + MLA_V, MLA_NOPE + np.arange(MLA_V)).T.astype(BF16),
        pq=_block_diag_mean([(MLA_NOPE, True), (MLA_ROPE, True), (MLA_SLOT - MLA_QK, False)], 256),
        pk=_block_diag_mean([(MLA_NOPE, True), (MLA_SLOT - MLA_NOPE, False)], 256),
        gq=_slot_vector([gq]),
        gkn=_slot_vector([gk[:MLA_NOPE]]),
        gkpe=_slot_vector([zeros_nope, gk[MLA_NOPE:]])[:, :MLA_SLOT],
        wo=wo.astype(BF16),
        bounded=_mla_logit_bound(mla_g_q[0], mla_g_k[0]) < FLASH_LOGIT_BOUND,
    )
    moe = []
    for i in range(w_router.shape[0]):
        moe.append(dict(
            g=g_norm2[i].reshape(1, D_MODEL),
            wrt=w_router[i].T,
            br=b_router[i].reshape(N_EXPERTS, 1),
            wg=w_gate[i].astype(BF16), wu=w_up[i].astype(BF16), wd=w_down[i].astype(BF16),
            wsg=ws_gate[i].astype(BF16), wsu=ws_up[i].astype(BF16), wsd=ws_down[i].astype(BF16),
        ))
    return na, mla, moe


def _moe_layer(x, mods, p):
    B, S, _ = x.shape
    half = D_MODEL // 2
    hn, eid, wts, rank, counts = _router(x, mods, p["g"], p["wrt"], p["br"])
    n_rows = B * S * TOP_K + N_EXPERTS * EXPERT_TILE
    rows, tile_expert, tile_live, n_tiles = _dispatch_plan(eid, rank, counts, n_rows)
    xs = _sc_scatter(hn.reshape(B * S, half), rows, n_rows)
    yield
    ys = _experts(xs, tile_expert, tile_live, n_tiles, p["wg"], p["wu"], p["wd"])
    yg = _sc_gather(ys, rows).reshape(TOP_K, B, S, half)
    yield
    return _combine(yg, jnp.swapaxes(wts, 1, 2), hn, p["wsg"], p["wsu"], p["wsd"], x, mods)


def _trunk(x, mods0, mods1, na, mla, moe):
    S = x.shape[1]
    q, k, v = _na_qkv(x, mods0, na["g"], na["w"], na["p"], na["gq"], na["gk"])
    o = _na_attn(q, k, v, na["bias"])
    x = _proj_res(o, na["wo"], x, mods0, 2)
    x = yield from _moe_layer(x, mods0, moe[0])
    cosf, sinf = _rope_tables(S)
    q, k, vt = _mla_proj(x, mods1, mla["g"], mla["wd"], mla["gcq"], mla["gckv"], mla["wuq"],
                         mla["wukv"], mla["wvt"], mla["pq"], mla["pk"], mla["gq"], mla["gkn"],
                         mla["gkpe"], cosf, sinf)
    o = _flash(q, k, vt, mla["bounded"])
    x = _proj_res(o, mla["wo"], x, mods1, 2)
    x = yield from _moe_layer(x, mods1, moe[1])
    return x


def _interleave(generators):
    results = [None] * len(generators)
    live = list(range(len(generators)))
    while live:
        for i in list(live):
            try:
                next(generators[i])
            except StopIteration as done:
                results[i] = done.value
                live.remove(i)
    return results


def kernel(x_prompt, x_sample, c_prompt, c_sample, g_norm1, g_norm2, w_ada, b_ada, na_w_qkv, na_g_q, na_g_k, na_rpb, na_w_o, mla_w_down, mla_g_cq, mla_g_ckv, mla_w_uq, mla_w_ukv, mla_g_q, mla_g_k, mla_w_o, w_router, b_router, w_gate, w_up, w_down, ws_gate, ws_up, ws_down):
    na, mla, moe = _prepare(g_norm1, g_norm2, na_w_qkv, na_g_q, na_g_k, na_rpb, na_w_o,
                            mla_w_down, mla_g_cq, mla_g_ckv, mla_w_uq, mla_w_ukv, mla_g_q, mla_g_k,
                            mla_w_o, w_router, b_router, w_gate, w_up, w_down,
                            ws_gate, ws_up, ws_down)
    bp, bs = c_prompt.shape[0], c_sample.shape[0]
    c_all = jnp.concatenate([c_prompt, c_sample], axis=0)
    c_all = jnp.pad(c_all, ((0, (-c_all.shape[0]) % 8), (0, 0)))
    mods = []
    for i in range(w_ada.shape[0]):
        m = _mods(c_all, w_ada[i], b_ada[i]).reshape(c_all.shape[0], 6, D_MODEL)
        mods.append(jnp.pad(m, ((0, 0), (0, 2), (0, 0))))
    y_prompt, y_sample = _interleave([
        _trunk(x_prompt, mods[0][:bp], mods[1][:bp], na, mla, moe),
        _trunk(x_sample, mods[0][bp:bp + bs], mods[1][bp:bp + bs], na, mla, moe)])
    return (y_prompt, y_sample)
```

```python
import functools
import math

import jax
import jax.numpy as jnp
import numpy as np
from jax import lax
from jax.experimental import pallas as pl
from jax.experimental.pallas import tpu as pltpu
from jax.experimental.pallas import tpu_sc as plsc

F32 = jnp.float32
BF16 = jnp.bfloat16

D_MODEL = 1024
GRID_W = 64
NA_HEADS = 16
NA_HEAD_DIM = 64
NA_WIN_H = 8
NA_WIN_W = 16
NA_Q_ROWS = 4
NA_K_ROWS = 3 * NA_Q_ROWS
MLA_HEADS = 16
MLA_Q_RANK = 384
MLA_KV_RANK = 256
MLA_NOPE = 64
MLA_ROPE = 32
MLA_V = 64
MLA_QK = MLA_NOPE + MLA_ROPE
MLA_SLOT = 128
ROPE_THETA = 10000.0
N_EXPERTS = 64
TOP_K = 6
N_GROUPS = 8
TOPK_GROUPS = 4
EXPERT_FF = 256
ROUTED_SCALE = 2.5
EXPERT_TILE = 512
SC_CORES = 2
SC_SUBCORES = 16
SC_WORKERS = SC_CORES * SC_SUBCORES
SC_GATHER_ROWS = 64
HIGH_HALF = -65536
EPS = 1e-6
NEG = -1e30
LOG2E = math.log2(math.e)
FLASH_LOGIT_BOUND = 60.0

VMEM_LIMIT = 56 * 1024 * 1024
NT_DIMS = (((1,), (1,)), ((), ()))


def _params(*sem):
    return pltpu.CompilerParams(dimension_semantics=sem, vmem_limit_bytes=VMEM_LIMIT)


def _sigmoid(x):
    return 1.0 / (1.0 + jnp.exp(-x))


def _modnorm(x, g, sc, sh):
    ms = jnp.mean(x * x, axis=-1, keepdims=True)
    return (x * lax.rsqrt(ms + EPS) * g) * (1.0 + sc) + sh


def _rmsnorm_rows(x, g):
    ms = jnp.mean(x * x, axis=-1, keepdims=True)
    return x * lax.rsqrt(ms + EPS) * g


def _segnorm(y, p, g):
    ms = jnp.dot((y * y).astype(BF16), p, preferred_element_type=F32)
    return y * lax.rsqrt(ms + EPS) * g


def _mods_kernel(c_ref, w_ref, b_ref, o_ref):
    c = c_ref[...]
    a = c * _sigmoid(c)
    o_ref[...] = jnp.dot(a, w_ref[...], preferred_element_type=F32,
                         precision=lax.Precision.HIGHEST) + b_ref[...]


def _mods(c_all, w, b):
    rows = c_all.shape[0]
    n = w.shape[1]
    tn = 1024
    return pl.pallas_call(
        _mods_kernel,
        out_shape=jax.ShapeDtypeStruct((rows, n), F32),
        grid=(n // tn,),
        in_specs=[pl.BlockSpec((rows, D_MODEL), lambda j: (0, 0)),
                  pl.BlockSpec((D_MODEL, tn), lambda j: (0, j)),
                  pl.BlockSpec((1, tn), lambda j: (0, j))],
        out_specs=pl.BlockSpec((rows, tn), lambda j: (0, j)),
        compiler_params=_params("arbitrary"), name="adaln_mods",
    )(c_all, w, b.reshape(1, n))


def _na_qkv_kernel(x_ref, mods_ref, g_ref, w_ref, p_ref, gq_ref, gk_ref, q_ref, k_ref, v_ref):
    hn = _modnorm(x_ref[...], g_ref[...], mods_ref[1:2, :], mods_ref[0:1, :]).astype(BF16)
    for part, o_ref, gg in ((0, q_ref, gq_ref), (1, k_ref, gk_ref), (2, v_ref, None)):
        for t in range(D_MODEL // 256):
            lo = t * 256
            y = jnp.dot(hn, w_ref[:, part * D_MODEL + lo:part * D_MODEL + lo + 256],
                        preferred_element_type=F32)
            if gg is not None:
                y = _segnorm(y, p_ref[...], gg[:, lo:lo + 256])
            o_ref[:, lo:lo + 256] = y.astype(BF16)


def _na_qkv(x, mods, g, w, p, gq, gk, tm=512):
    B, S, _ = x.shape
    tok = pl.BlockSpec((None, tm, D_MODEL), lambda b, i: (b, i, 0))
    full = lambda a: pl.BlockSpec(a.shape, lambda b, i: (0,) * a.ndim)
    out = jax.ShapeDtypeStruct((B, S, D_MODEL), BF16)
    return pl.pallas_call(
        _na_qkv_kernel,
        out_shape=(out, out, out),
        grid=(B, S // tm),
        in_specs=[tok, pl.BlockSpec((None, 8, D_MODEL), lambda b, i: (b, 0, 0)),
                  full(g), full(w), full(p), full(gq), full(gk)],
        out_specs=(tok, tok, tok),
        compiler_params=_params("parallel", "parallel"), name="na_qkv",
    )(x, mods, g, w, p, gq, gk)


def _na_attn_kernel(q_ref, k0_ref, k1_ref, k2_ref, v0_ref, v1_ref, v2_ref, bias_ref, o_ref, *, bounded):
    lane = lax.broadcasted_iota(jnp.int32, (1, 128), 1)
    first = lane < NA_HEAD_DIM
    for hp in range(NA_HEADS // 2):
        cols = slice(hp * 128, (hp + 1) * 128)
        q = q_ref[:, cols]
        k = jnp.concatenate([k0_ref[:, cols], k1_ref[:, cols], k2_ref[:, cols]], axis=0)
        v = jnp.concatenate([v0_ref[:, cols], v1_ref[:, cols], v2_ref[:, cols]], axis=0)
        if bounded:
            v = jnp.concatenate([v, jnp.ones_like(v)], axis=1)
        outs = []
        for sub in range(2):
            keep = first if sub == 0 else jnp.logical_not(first)
            qm = jnp.where(keep, q, jnp.zeros_like(q))
            s = lax.dot_general(qm, k, NT_DIMS, preferred_element_type=F32)
            s = s + bias_ref[2 * hp + sub].astype(F32)
            if bounded:
                o = jnp.dot(jnp.exp2(s).astype(BF16), v, preferred_element_type=F32)
                outs.append(o[:, :128] / o[:, 128:129])
            else:
                e = jnp.exp2(s - jnp.max(s, axis=-1, keepdims=True))
                o = jnp.dot(e.astype(BF16), v, preferred_element_type=F32)
                outs.append(o / jnp.sum(e, axis=-1, keepdims=True))
        o_ref[:, cols] = jnp.where(first, outs[0], outs[1]).astype(BF16)


def _na_attn(q, k, v, bias, bounded):
    return lax.cond(bounded, functools.partial(_na_attn_call, bounded=True),
                    functools.partial(_na_attn_call, bounded=False), q, k, v, bias)


def _na_logit_bound(g_q, g_k, rpb):
    qk = NA_HEAD_DIM * jnp.max(jnp.abs(g_q)) * jnp.max(jnp.abs(g_k)) * NA_HEAD_DIM ** -0.5
    return 1.05 * (qk + jnp.max(jnp.abs(rpb))) * LOG2E


def _na_attn_call(q, k, v, bias, *, bounded):
    B, S, _ = q.shape
    tq = NA_Q_ROWS * GRID_W
    ng = S // tq
    assert ng >= 3

    def kv_spec(j):
        return pl.BlockSpec((None, tq, D_MODEL),
                            lambda b, g: (b, jnp.clip(g - 1, 0, ng - 3) + j, 0))

    def bias_map(b, g):
        return (0, jnp.where(g == 0, 0, jnp.where(g == ng - 1, 2, 1)), 0)

    qspec = pl.BlockSpec((None, tq, D_MODEL), lambda b, g: (b, g, 0))
    return pl.pallas_call(
        functools.partial(_na_attn_kernel, bounded=bounded),
        out_shape=jax.ShapeDtypeStruct((B, S, D_MODEL), BF16),
        grid=(B, ng),
        in_specs=[qspec, kv_spec(0), kv_spec(1), kv_spec(2), kv_spec(0), kv_spec(1), kv_spec(2),
                  pl.BlockSpec((NA_HEADS, tq, 3 * tq), bias_map)],
        out_specs=qspec,
        compiler_params=_params("parallel", "arbitrary"),
        name="na_attn_bounded" if bounded else "na_attn",
    )(q, k, k, k, v, v, v, bias)


def _na_bias_table(rpb):
    rows = NA_K_ROWS
    r = np.arange(rows)
    c = np.arange(GRID_W)
    rs = np.clip(r - NA_WIN_H // 2, 0, rows - NA_WIN_H)
    cs = np.clip(c - NA_WIN_W // 2, 0, GRID_W - NA_WIN_W)
    dr = r[None, :] - r[:, None]
    dc = c[None, :] - c[:, None]
    ok_r = (r[None, :] >= rs[:, None]) & (r[None, :] < rs[:, None] + NA_WIN_H)
    ok_c = (c[None, :] >= cs[:, None]) & (c[None, :] < cs[:, None] + NA_WIN_W)
    ri = np.clip(dr + NA_WIN_H - 1, 0, 2 * NA_WIN_H - 2)
    ci = np.clip(dc + NA_WIN_W - 1, 0, 2 * NA_WIN_W - 2)
    ok = ok_r[:, None, :, None] & ok_c[None, :, None, :]
    sel_r = jnp.asarray(ri[:, :, None] == np.arange(2 * NA_WIN_H - 1), F32)
    sel_c = jnp.asarray(ci[:, :, None] == np.arange(2 * NA_WIN_W - 1), F32)
    by_row = jnp.einsum('hij,qki->hqkj', rpb, sel_r, precision=lax.Precision.HIGHEST)
    tab = jnp.einsum('hqkj,cdj->hqckd', by_row, sel_c, precision=lax.Precision.HIGHEST)
    tab = jnp.where(ok[None], tab * LOG2E, NEG)
    return tab.reshape(NA_HEADS, rows * GRID_W, rows * GRID_W).astype(BF16)


def _proj_res_kernel(o_ref, w_ref, x_ref, mods_ref, out_ref, *, gate_row):
    y = jnp.dot(o_ref[...], w_ref[...], preferred_element_type=F32)
    out_ref[...] = x_ref[...] + mods_ref[gate_row:gate_row + 1, :] * y


def _proj_res(o, w, x, mods, gate_row, tm=512):
    B, S, K = o.shape
    tok = pl.BlockSpec((None, tm, D_MODEL), lambda b, i: (b, i, 0))
    return pl.pallas_call(
        functools.partial(_proj_res_kernel, gate_row=gate_row),
        out_shape=jax.ShapeDtypeStruct((B, S, D_MODEL), F32),
        grid=(B, S // tm),
        in_specs=[pl.BlockSpec((None, tm, K), lambda b, i: (b, i, 0)),
                  pl.BlockSpec(w.shape, lambda b, i: (0, 0)),
                  tok, pl.BlockSpec((None, 8, D_MODEL), lambda b, i: (b, 0, 0))],
        out_specs=tok,
        compiler_params=_params("parallel", "parallel"), name="proj_res",
    )(o, w, x, mods)


def _rope(y, cosf, sinf, width):
    lane = lax.broadcasted_iota(jnp.int32, (1, width), 1) % MLA_SLOT
    half = MLA_ROPE // 2
    up = pltpu.roll(y, width - half, 1)
    dn = pltpu.roll(y, half, 1)
    lo = (lane >= MLA_NOPE) & (lane < MLA_NOPE + half)
    hi = (lane >= MLA_NOPE + half) & (lane < MLA_QK)
    rot = jnp.where(lo, -up, jnp.where(hi, dn, 0.0))
    return y * cosf + rot * sinf


def _mla_proj_kernel(x_ref, mods_ref, g_ref, wd_ref, gcq_ref, gckv_ref, wuq_ref, wukv_ref, wvt_ref,
                     pq_ref, pk_ref, gq_ref, gkn_ref, gkpe_ref, cos_ref, sin_ref,
                     q_ref, k_ref, vt_ref):
    hn = _modnorm(x_ref[...], g_ref[...], mods_ref[1:2, :], mods_ref[0:1, :]).astype(BF16)
    down = jnp.dot(hn, wd_ref[...], preferred_element_type=F32)
    cq = _rmsnorm_rows(down[:, :MLA_Q_RANK], gcq_ref[...]).astype(BF16)
    ckv = _rmsnorm_rows(down[:, MLA_Q_RANK:MLA_Q_RANK + MLA_KV_RANK], gckv_ref[...]).astype(BF16)
    cosf = cos_ref[...]
    sinf = sin_ref[...]
    kpe = down[:, MLA_Q_RANK + MLA_KV_RANK:]
    kpe = _segnorm(kpe, pq_ref[:MLA_SLOT, :MLA_SLOT], gkpe_ref[...])
    kpe = _rope(kpe, cosf[:, :MLA_SLOT], sinf[:, :MLA_SLOT], MLA_SLOT)
    kpe2 = jnp.concatenate([kpe, kpe], axis=1)
    sub = lax.broadcasted_iota(jnp.int32, (256, 1), 0) % MLA_SLOT
    ones_row = jnp.where(sub == MLA_V, 1.0, 0.0)
    n_tiles = MLA_HEADS * MLA_SLOT // 256
    for t in range(n_tiles):
        cols = slice(t * 256, (t + 1) * 256)
        y = jnp.dot(cq, wuq_ref[:, cols], preferred_element_type=F32)
        y = _segnorm(y, pq_ref[...], gq_ref[...])
        q_ref[:, cols] = _rope(y, cosf, sinf, 256).astype(BF16)
    for t in range(n_tiles):
        cols = slice(t * 256, (t + 1) * 256)
        y = jnp.dot(ckv, wukv_ref[:, cols], preferred_element_type=F32)
        y = _segnorm(y, pk_ref[...], gkn_ref[...])
        k_ref[:, cols] = (y + kpe2).astype(BF16)
    for t in range(n_tiles):
        rows = slice(t * 256, (t + 1) * 256)
        yt = lax.dot_general(wvt_ref[rows, :], ckv, NT_DIMS, preferred_element_type=F32)
        vt_ref[rows, :] = (yt + ones_row).astype(BF16)


def _mla_proj(x, mods, g, wd, gcq, gckv, wuq, wukv, wvt, pq, pk, gq, gkn, gkpe, cosf, sinf, tm=512):
    B, S, _ = x.shape
    wide = MLA_HEADS * MLA_SLOT
    tok = pl.BlockSpec((None, tm, D_MODEL), lambda b, i: (b, i, 0))
    full = lambda a: pl.BlockSpec(a.shape, lambda b, i: (0,) * a.ndim)
    pos = pl.BlockSpec((tm, 256), lambda b, i: (i, 0))
    out = jax.ShapeDtypeStruct((B, S, wide), BF16)
    ospec = pl.BlockSpec((None, tm, wide), lambda b, i: (b, i, 0))
    return pl.pallas_call(
        _mla_proj_kernel,
        out_shape=(out, out, jax.ShapeDtypeStruct((B, wide, S), BF16)),
        grid=(B, S // tm),
        in_specs=[tok, pl.BlockSpec((None, 8, D_MODEL), lambda b, i: (b, 0, 0)),
                  full(g), full(wd), full(gcq), full(gckv), full(wuq), full(wukv), full(wvt),
                  full(pq), full(pk), full(gq), full(gkn), full(gkpe), pos, pos],
        out_specs=(ospec, ospec, pl.BlockSpec((None, wide, tm), lambda b, i: (b, 0, i))),
        compiler_params=_params("parallel", "parallel"), name="mla_proj",
    )(x, mods, g, wd, gcq, gckv, wuq, wukv, wvt, pq, pk, gq, gkn, gkpe, cosf, sinf)


def _flash_kernel(q_ref, k_ref, vt_ref, o_ref, *, tk):
    q = q_ref[...]
    tq = q.shape[0]
    nk = k_ref.shape[0] // tk

    def body(j, carry):
        m, acc = carry
        ks = pl.multiple_of(j * tk, tk)
        k = k_ref[pl.ds(ks, tk), :]
        vt = vt_ref[:, pl.ds(ks, tk)]
        s = lax.dot_general(q, k, NT_DIMS, preferred_element_type=F32)
        m_new = jnp.maximum(m, jnp.max(s, axis=-1, keepdims=True))
        alpha = jnp.exp2(m - m_new)
        p = jnp.exp2(s - m_new)
        acc = alpha * acc + lax.dot_general(p.astype(BF16), vt, NT_DIMS, preferred_element_type=F32)
        return m_new, acc

    m0 = jnp.full((tq, 1), -jnp.inf, F32)
    acc0 = jnp.zeros((tq, MLA_SLOT), F32)
    _, acc = lax.fori_loop(0, nk, body, (m0, acc0))
    o_ref[...] = (acc / acc[:, MLA_V:MLA_V + 1]).astype(BF16)


def _flash_bounded_kernel(q_ref, k_ref, vt_ref, o_ref, acc_ref, *, tk, sub):
    nk = k_ref.shape[0] // tk
    acc_ref[...] = jnp.zeros_like(acc_ref)
    q = q_ref[...]

    def body(j, carry):
        for a in range(tk // sub):
            ks = pl.multiple_of(j * tk + a * sub, sub)
            st = lax.dot_general(k_ref[pl.ds(ks, sub), :], q, NT_DIMS, preferred_element_type=F32)
            acc_ref[...] += jnp.dot(vt_ref[:, pl.ds(ks, sub)], jnp.exp2(st).astype(BF16),
                                    preferred_element_type=F32)
        return carry

    lax.fori_loop(0, nk, body, 0, unroll=2)
    acc = acc_ref[...]
    o_ref[...] = jnp.transpose(acc / acc[MLA_V:MLA_V + 1, :]).astype(BF16)


def _flash_call(kernel_fn, q, k, vt, tq, scratch, name):
    B, S, wide = q.shape
    H = wide // MLA_SLOT
    qspec = pl.BlockSpec((None, tq, MLA_SLOT), lambda b, h, i: (b, i, h))
    kspec = pl.BlockSpec((None, S, MLA_SLOT), lambda b, h, i: (b, 0, h))
    vtspec = pl.BlockSpec((None, MLA_SLOT, S), lambda b, h, i: (b, h, 0))
    return pl.pallas_call(
        kernel_fn,
        out_shape=jax.ShapeDtypeStruct((B, S, wide), BF16),
        grid=(B, H, S // tq),
        in_specs=[qspec, kspec, vtspec],
        out_specs=qspec,
        scratch_shapes=scratch,
        compiler_params=_params("parallel", "parallel", "arbitrary"), name=name,
    )(q, k, vt)


def _flash(q, k, vt, bounded):
    fast = functools.partial(
        _flash_call, functools.partial(_flash_bounded_kernel, tk=1024, sub=512),
        tq=2048, scratch=[pltpu.VMEM((MLA_SLOT, 2048), F32)], name="mla_flash_bounded")
    general = functools.partial(
        _flash_call, functools.partial(_flash_kernel, tk=512),
        tq=512, scratch=[], name="mla_flash")
    return lax.cond(bounded, fast, general, q, k, vt)


def _mla_logit_bound(g_q, g_k):
    def seg(g):
        return MLA_NOPE * jnp.max(jnp.abs(g[:MLA_NOPE])) ** 2 + MLA_ROPE * jnp.max(jnp.abs(g[MLA_NOPE:])) ** 2
    return 1.05 * jnp.sqrt(seg(g_q) * seg(g_k)) * MLA_QK ** -0.5 * LOG2E


def _pack_halves(y):
    w = y.shape[1] // 2
    lo = pltpu.bitcast(y[:, :w].astype(BF16).astype(F32), jnp.int32)
    hi = pltpu.bitcast(y[:, w:].astype(BF16).astype(F32), jnp.int32)
    return (hi & HIGH_HALF) | lax.shift_right_logical(lo, 16)


def _unpack_halves(p):
    return pltpu.bitcast(p << 16, F32), pltpu.bitcast(p & HIGH_HALF, F32)


def _router_kernel(x_ref, mods_ref, g_ref, wrt_ref, br_ref, tri_ref,
                   hn_ref, eid_ref, wts_ref, rank_ref, cnt_ref, run_ref):
    @pl.when((pl.program_id(0) == 0) & (pl.program_id(1) == 0))
    def _():
        run_ref[...] = jnp.zeros_like(run_ref)

    hn = _modnorm(x_ref[...], g_ref[...], mods_ref[4:5, :], mods_ref[3:4, :])
    hn_ref[...] = _pack_halves(hn)
    tm = hn.shape[0]
    logits = lax.dot_general(wrt_ref[...], hn, NT_DIMS, preferred_element_type=F32,
                             precision=lax.Precision.HIGHEST)
    scores = _sigmoid(logits)
    choice = scores + br_ref[...]
    per = N_EXPERTS // N_GROUPS
    sub8 = lax.broadcasted_iota(jnp.int32, (per, tm), 0).astype(F32)
    gs = []
    for gi in range(N_GROUPS):
        cg = choice[gi * per:(gi + 1) * per, :]
        m1 = jnp.max(cg, axis=0, keepdims=True)
        i1 = jnp.min(jnp.where(cg == m1, sub8, float(per)), axis=0, keepdims=True)
        m2 = jnp.max(jnp.where(sub8 == i1, -jnp.inf, cg), axis=0, keepdims=True)
        gs.append(m1 + m2)
    gs8 = jnp.concatenate(gs, axis=0)
    rank = jnp.zeros_like(gs8)
    for gi in range(N_GROUPS):
        row = gs8[gi:gi + 1, :]
        ahead = (row > gs8) | ((row == gs8) & (sub8 > float(gi)))
        rank = rank + jnp.where(ahead, 1.0, 0.0)
    cur = jnp.concatenate(
        [jnp.where(rank[gi:gi + 1, :] < float(TOPK_GROUPS),
                   choice[gi * per:(gi + 1) * per, :], -jnp.inf)
         for gi in range(N_GROUPS)], axis=0)
    sub64 = lax.broadcasted_iota(jnp.int32, (N_EXPERTS, tm), 0).astype(F32)
    hits = []
    for _ in range(TOP_K):
        m = jnp.max(cur, axis=0, keepdims=True)
        idx = jnp.min(jnp.where(cur == m, sub64, float(N_EXPERTS)), axis=0, keepdims=True)
        hit = sub64 == idx
        hits.append((hit, idx))
        cur = jnp.where(hit, -jnp.inf, cur)
    chosen = jnp.zeros_like(cur)
    for hit, _ in hits:
        chosen = jnp.where(hit, 1.0, chosen)
    prefix = jnp.dot(chosen.astype(BF16), tri_ref[...], preferred_element_type=F32)
    rank_full = run_ref[:, 0:1] + prefix
    slot = lax.broadcasted_iota(jnp.int32, (8, tm), 0)
    eid8 = jnp.zeros((8, tm), F32)
    rank8 = jnp.zeros((8, tm), F32)
    w8 = jnp.zeros((8, tm), F32)
    for k, (hit, idx) in enumerate(hits):
        eid8 = jnp.where(slot == k, idx, eid8)
        rank8 = jnp.where(slot == k, jnp.sum(jnp.where(hit, rank_full, 0.0), axis=0, keepdims=True), rank8)
        w8 = jnp.where(slot == k, jnp.sum(jnp.where(hit, scores, 0.0), axis=0, keepdims=True), w8)
    eid_ref[...] = eid8.astype(jnp.int32)
    rank_ref[...] = rank8.astype(jnp.int32)
    wts_ref[...] = w8 / jnp.sum(w8, axis=0, keepdims=True) * ROUTED_SCALE
    run_ref[...] = run_ref[...] + jnp.sum(chosen, axis=1, keepdims=True)
    cnt_ref[...] = run_ref[...]


def _router(x, mods, g, wrt, br, tm=512):
    B, S, _ = x.shape
    tok = pl.BlockSpec((None, tm, D_MODEL), lambda b, i: (b, i, 0))
    slots = pl.BlockSpec((None, 8, tm), lambda b, i: (b, 0, i))
    tri = jnp.asarray(np.triu(np.ones((tm, tm), np.float32), 1), BF16)
    const = lambda a: pl.BlockSpec(a.shape, lambda b, i: (0, 0))
    return pl.pallas_call(
        _router_kernel,
        out_shape=(jax.ShapeDtypeStruct((B, S, D_MODEL // 2), jnp.int32),
                   jax.ShapeDtypeStruct((B, 8, S), jnp.int32),
                   jax.ShapeDtypeStruct((B, 8, S), F32),
                   jax.ShapeDtypeStruct((B, 8, S), jnp.int32),
                   jax.ShapeDtypeStruct((N_EXPERTS, 128), F32)),
        grid=(B, S // tm),
        in_specs=[tok, pl.BlockSpec((None, 8, D_MODEL), lambda b, i: (b, 0, 0)),
                  const(g), const(wrt), const(br), const(tri)],
        out_specs=(pl.BlockSpec((None, tm, D_MODEL // 2), lambda b, i: (b, i, 0)),
                   slots, slots, slots,
                   pl.BlockSpec((N_EXPERTS, 128), lambda b, i: (0, 0))),
        scratch_shapes=[pltpu.VMEM((N_EXPERTS, 128), F32)],
        compiler_params=_params("arbitrary", "arbitrary"), name="moe_router",
    )(x, mods, g, wrt, br, tri)


def _sc_gather(table, idx):
    R = idx.shape[0]
    W = table.shape[1]
    per_w = R // SC_WORKERS
    steps = per_w // SC_GATHER_ROWS
    assert per_w * SC_WORKERS == R and steps * SC_GATHER_ROWS == per_w and steps % 2 == 0
    idx3 = idx.reshape(SC_WORKERS, steps, SC_GATHER_ROWS)
    mesh = plsc.VectorSubcoreMesh(core_axis_name="c", subcore_axis_name="s")

    @functools.partial(
        pl.kernel, mesh=mesh,
        out_type=jax.ShapeDtypeStruct((R, W), jnp.int32),
        scratch_types=[pltpu.VMEM((steps, SC_GATHER_ROWS), jnp.int32),
                       pltpu.VMEM((2, SC_GATHER_ROWS, W), jnp.int32),
                       pltpu.SemaphoreType.DMA((2,)),
                       pltpu.SemaphoreType.DMA((2,))],
    )
    def gather_kernel(table_hbm, idx_hbm, out_hbm, idx_v, rows_v, gsem, wsem):
        wid = lax.axis_index("s") * SC_CORES + lax.axis_index("c")
        base = wid * per_w
        pltpu.sync_copy(idx_hbm.at[wid], idx_v)

        def gather(j, b):
            return pltpu.make_async_copy(table_hbm.at[idx_v.at[j]], rows_v.at[b], gsem.at[b])

        def write(j, b):
            return pltpu.make_async_copy(
                rows_v.at[b], out_hbm.at[pl.ds(base + j * SC_GATHER_ROWS, SC_GATHER_ROWS)], wsem.at[b])

        gather(0, 0).start()
        gather(1, 1).start()

        @pl.loop(0, steps, step=2)
        def _(j):
            for b in range(2):
                gather(j + b, b).wait()
                write(j + b, b).start()
                write(j + b, b).wait()

                @pl.when(j + b + 2 < steps)
                def _():
                    gather(j + b + 2, b).start()

    return gather_kernel(table, idx3)


def _sc_scatter(rows, idx, n_out):
    A = idx.shape[0]
    V, W = rows.shape
    per_w = A // SC_WORKERS
    steps = per_w // SC_GATHER_ROWS
    assert per_w * SC_WORKERS == A and steps * SC_GATHER_ROWS == per_w and steps % 2 == 0
    assert V % SC_GATHER_ROWS == 0
    idx3 = idx.reshape(SC_WORKERS, steps, SC_GATHER_ROWS)
    mesh = plsc.VectorSubcoreMesh(core_axis_name="c", subcore_axis_name="s")

    @functools.partial(
        pl.kernel, mesh=mesh,
        out_type=jax.ShapeDtypeStruct((n_out, W), jnp.int32),
        scratch_types=[pltpu.VMEM((steps, SC_GATHER_ROWS), jnp.int32),
                       pltpu.VMEM((2, SC_GATHER_ROWS, W), jnp.int32),
                       pltpu.SemaphoreType.DMA((2,)),
                       pltpu.SemaphoreType.DMA((2,))],
    )
    def scatter_kernel(rows_hbm, idx_hbm, out_hbm, idx_v, buf_v, rsem, wsem):
        wid = lax.axis_index("s") * SC_CORES + lax.axis_index("c")
        base = wid * per_w
        pltpu.sync_copy(idx_hbm.at[wid], idx_v)

        def read(j, b):
            src = lax.rem(base + j * SC_GATHER_ROWS, V)
            return pltpu.make_async_copy(rows_hbm.at[pl.ds(src, SC_GATHER_ROWS)], buf_v.at[b], rsem.at[b])

        def write(j, b):
            return pltpu.make_async_copy(buf_v.at[b], out_hbm.at[idx_v.at[j]], wsem.at[b])

        read(0, 0).start()
        read(1, 1).start()

        @pl.loop(0, steps, step=2)
        def _(j):
            for b in range(2):
                read(j + b, b).wait()
                write(j + b, b).start()
                write(j + b, b).wait()

                @pl.when(j + b + 2 < steps)
                def _():
                    read(j + b + 2, b).start()

    return scatter_kernel(rows, idx3)


def _dispatch_plan(eid, rank, counts, n_rows):
    eid = eid[:, :TOP_K, :]
    rank = rank[:, :TOP_K, :]
    counts = counts[:, 0].astype(jnp.int32)
    padded = (counts + EXPERT_TILE - 1) // EXPERT_TILE * EXPERT_TILE
    ends = jnp.cumsum(padded)
    offs = ends - padded
    experts = jnp.arange(N_EXPERTS, dtype=jnp.int32)
    row = rank + jnp.sum(jnp.where(eid[..., None] == experts, offs, 0), axis=-1)
    row_slotmajor = jnp.transpose(row, (1, 0, 2)).reshape(-1)
    tile_start = jnp.arange(n_rows // EXPERT_TILE, dtype=jnp.int32) * EXPERT_TILE
    tile_expert = jnp.minimum(jnp.sum(tile_start[:, None] >= ends[None, :], axis=1), N_EXPERTS - 1)
    live_end = jnp.sum(jnp.where(tile_expert[:, None] == experts, offs + counts, 0), axis=1)
    tile_live = jnp.clip(live_end - tile_start, 0, EXPERT_TILE)
    n_tiles = (ends[-1] // EXPERT_TILE).reshape(1)
    return (row_slotmajor, tile_expert.astype(jnp.int32), tile_live.astype(jnp.int32),
            n_tiles.astype(jnp.int32))


def _experts_kernel(te_ref, tl_ref, nt_ref, xs_ref, wgu_ref, wd_ref, y_ref):
    i = pl.program_id(0)

    @pl.when(i < nt_ref[0])
    def _():
        live = lax.broadcasted_iota(jnp.int32, (EXPERT_TILE, 1), 0) < tl_ref[i]
        lo, hi = _unpack_halves(jnp.where(live, xs_ref[...], 0))
        x = jnp.concatenate([lo.astype(BF16), hi.astype(BF16)], axis=1)
        au = jnp.dot(x, wgu_ref[...], preferred_element_type=F32)
        a = au[:, :EXPERT_FF]
        h = ((a * _sigmoid(a)) * au[:, EXPERT_FF:]).astype(BF16)
        y_ref[...] = _pack_halves(jnp.dot(h, wd_ref[...], preferred_element_type=F32))

    @pl.when(i >= nt_ref[0])
    def _():
        y_ref[...] = jnp.zeros_like(y_ref)


def _experts(xs, tile_expert, tile_live, n_tiles, wgu, wd):
    n_rows, half = xs.shape
    rows = pl.BlockSpec((EXPERT_TILE, half), lambda i, te, tl, nt: (i, 0))
    return pl.pallas_call(
        _experts_kernel,
        out_shape=jax.ShapeDtypeStruct((n_rows, half), jnp.int32),
        grid_spec=pltpu.PrefetchScalarGridSpec(
            num_scalar_prefetch=3,
            grid=(n_rows // EXPERT_TILE,),
            in_specs=[rows,
                      pl.BlockSpec((None, D_MODEL, 2 * EXPERT_FF), lambda i, te, tl, nt: (te[i], 0, 0)),
                      pl.BlockSpec((None, EXPERT_FF, D_MODEL), lambda i, te, tl, nt: (te[i], 0, 0))],
            out_specs=rows),
        compiler_params=_params("arbitrary"), name="moe_experts",
    )(tile_expert, tile_live, n_tiles, xs, wgu, wd)


def _combine_kernel(yg_ref, wts_ref, hn_ref, wsg_ref, wsu_ref, wsd_ref, x_ref, mods_ref, out_ref):
    lo, hi = _unpack_halves(hn_ref[...])
    hn = jnp.concatenate([lo.astype(BF16), hi.astype(BF16)], axis=1)
    a = jnp.dot(hn, wsg_ref[...], preferred_element_type=F32)
    u = jnp.dot(hn, wsu_ref[...], preferred_element_type=F32)
    shared = jnp.dot(((a * _sigmoid(a)) * u).astype(BF16), wsd_ref[...], preferred_element_type=F32)
    half = D_MODEL // 2
    acc_lo = shared[:, :half]
    acc_hi = shared[:, half:]
    wts = wts_ref[...]
    for k in range(TOP_K):
        lo, hi = _unpack_halves(yg_ref[k])
        wk = wts[:, k:k + 1]
        acc_lo = acc_lo + wk * lo
        acc_hi = acc_hi + wk * hi
    g2 = mods_ref[5:6, :]
    out_ref[:, :half] = x_ref[:, :half] + g2[:, :half] * acc_lo
    out_ref[:, half:] = x_ref[:, half:] + g2[:, half:] * acc_hi


def _combine(yg, wts, hn, wsg, wsu, wsd, x, mods, tm=512):
    B, S, _ = x.shape
    half = D_MODEL // 2
    tok = pl.BlockSpec((None, tm, D_MODEL), lambda b, i: (b, i, 0))
    const = lambda a: pl.BlockSpec(a.shape, lambda b, i: (0,) * a.ndim)
    return pl.pallas_call(
        _combine_kernel,
        out_shape=jax.ShapeDtypeStruct((B, S, D_MODEL), F32),
        grid=(B, S // tm),
        in_specs=[pl.BlockSpec((TOP_K, None, tm, half), lambda b, i: (0, b, i, 0)),
                  pl.BlockSpec((None, tm, 8), lambda b, i: (b, i, 0)),
                  pl.BlockSpec((None, tm, half), lambda b, i: (b, i, 0)),
                  const(wsg), const(wsu), const(wsd),
                  tok, pl.BlockSpec((None, 8, D_MODEL), lambda b, i: (b, 0, 0))],
        out_specs=tok,
        compiler_params=_params("parallel", "parallel"), name="moe_combine",
    )(yg, wts, hn, wsg, wsu, wsd, x, mods)


def _block_diag_mean(segments, width):
    p = np.zeros((width, width), np.float32)
    pos = 0
    while pos < width:
        for length, live in segments:
            if live:
                p[pos:pos + length, pos:pos + length] = 1.0 / length
            pos += length
    return jnp.asarray(p, BF16)


def _slot_layout(w, n_heads, per_head, take, slot=MLA_SLOT):
    k = w.shape[0]
    wh = w.reshape(k, n_heads, per_head)[:, :, take]
    wh = jnp.pad(wh, ((0, 0), (0, 0), (0, slot - wh.shape[-1])))
    return wh.reshape(k, n_heads * slot)


def _slot_vector(pieces):
    v = jnp.concatenate(pieces)
    v = jnp.pad(v, (0, MLA_SLOT - v.shape[0]))
    return jnp.tile(v, 2).reshape(1, 2 * MLA_SLOT)


def _rope_tables(S):
    half = MLA_ROPE // 2
    inv = 1.0 / (ROPE_THETA ** (jnp.arange(half, dtype=F32) / half))
    ang = jnp.arange(S, dtype=F32)[:, None] * inv[None, :]
    cos, sin = jnp.cos(ang), jnp.sin(ang)
    ones = jnp.ones((S, MLA_NOPE), F32)
    pad = jnp.zeros((S, MLA_SLOT - MLA_QK), F32)
    cosf = jnp.concatenate([ones, cos, cos, pad], axis=1)
    sinf = jnp.concatenate([0.0 * ones, sin, sin, pad], axis=1)
    return jnp.tile(cosf, (1, 2)), jnp.tile(sinf, (1, 2))


def _prepare(g_norm1, g_norm2, na_w_qkv, na_g_q, na_g_k, na_rpb, na_w_o,
             mla_w_down, mla_g_cq, mla_g_ckv, mla_w_uq, mla_w_ukv, mla_g_q, mla_g_k, mla_w_o,
             w_router, b_router, w_gate, w_up, w_down, ws_gate, ws_up, ws_down):
    na = dict(
        g=g_norm1[0].reshape(1, D_MODEL),
        w=na_w_qkv[0].astype(BF16),
        p=_block_diag_mean([(NA_HEAD_DIM, True)], 256),
        gq=(jnp.tile(na_g_q[0], NA_HEADS) * (NA_HEAD_DIM ** -0.5 * LOG2E)).reshape(1, D_MODEL),
        gk=jnp.tile(na_g_k[0], NA_HEADS).reshape(1, D_MODEL),
        bias=_na_bias_table(na_rpb[0]),
        bounded=_na_logit_bound(na_g_q[0], na_g_k[0], na_rpb[0]) < FLASH_LOGIT_BOUND,
        wo=na_w_o[0].astype(BF16),
    )
    wdn = mla_w_down[0]
    kpe_cols = jnp.pad(wdn[:, MLA_Q_RANK + MLA_KV_RANK:],
                       ((0, 0), (MLA_NOPE, MLA_SLOT - MLA_QK)))
    nope = np.arange(MLA_NOPE)
    wuq = mla_w_uq[0]
    wukv = mla_w_ukv[0]
    wo = mla_w_o[0].reshape(MLA_HEADS, MLA_V, D_MODEL)
    wo = jnp.pad(wo, ((0, 0), (0, MLA_SLOT - MLA_V), (0, 0))).reshape(MLA_HEADS * MLA_SLOT, D_MODEL)
    gq = mla_g_q[0] * (MLA_QK ** -0.5 * LOG2E)
    gk = mla_g_k[0]
    zeros_nope = jnp.zeros((MLA_NOPE,), F32)
    mla = dict(
        g=g_norm1[1].reshape(1, D_MODEL),
        wd=jnp.concatenate([wdn[:, :MLA_Q_RANK + MLA_KV_RANK], kpe_cols], axis=1).astype(BF16),
        gcq=mla_g_cq[0].reshape(1, MLA_Q_RANK),
        gckv=mla_g_ckv[0].reshape(1, MLA_KV_RANK),
        wuq=_slot_layout(wuq, MLA_HEADS, MLA_QK, np.arange(MLA_QK)).astype(BF16),
        wukv=_slot_layout(wukv, MLA_HEADS, MLA_NOPE + MLA_V, nope).astype(BF16),
        wvt=_slot_layout(wukv, MLA_HEADS, MLA_NOPE + MLA_V, MLA_NOPE + np.arange(MLA_V)).T.astype(BF16),
        pq=_block_diag_mean([(MLA_NOPE, True), (MLA_ROPE, True), (MLA_SLOT - MLA_QK, False)], 256),
        pk=_block_diag_mean([(MLA_NOPE, True), (MLA_SLOT - MLA_NOPE, False)], 256),
        gq=_slot_vector([gq]),
        gkn=_slot_vector([gk[:MLA_NOPE]]),
        gkpe=_slot_vector([zeros_nope, gk[MLA_NOPE:]])[:, :MLA_SLOT],
        wo=wo.astype(BF16),
        bounded=_mla_logit_bound(mla_g_q[0], mla_g_k[0]) < FLASH_LOGIT_BOUND,
    )
    moe = []
    for i in range(w_router.shape[0]):
        moe.append(dict(
            g=g_norm2[i].reshape(1, D_MODEL),
            wrt=w_router[i].T,
            br=b_router[i].reshape(N_EXPERTS, 1),
            wgu=jnp.concatenate([w_gate[i], w_up[i]], axis=2).astype(BF16), wd=w_down[i].astype(BF16),
            wsg=ws_gate[i].astype(BF16), wsu=ws_up[i].astype(BF16), wsd=ws_down[i].astype(BF16),
        ))
    return na, mla, moe


def _moe_layer(x, mods, p):
    B, S, _ = x.shape
    half = D_MODEL // 2
    hn, eid, wts, rank, counts = _router(x, mods, p["g"], p["wrt"], p["br"])
    n_rows = B * S * TOP_K + N_EXPERTS * EXPERT_TILE
    rows, tile_expert, tile_live, n_tiles = _dispatch_plan(eid, rank, counts, n_rows)
    xs = _sc_scatter(hn.reshape(B * S, half), rows, n_rows)
    yield
    ys = _experts(xs, tile_expert, tile_live, n_tiles, p["wgu"], p["wd"])
    yg = _sc_gather(ys, rows).reshape(TOP_K, B, S, half)
    yield
    return _combine(yg, jnp.swapaxes(wts, 1, 2), hn, p["wsg"], p["wsu"], p["wsd"], x, mods)


def _trunk(x, mods0, mods1, na, mla, moe):
    S = x.shape[1]
    q, k, v = _na_qkv(x, mods0, na["g"], na["w"], na["p"], na["gq"], na["gk"])
    o = _na_attn(q, k, v, na["bias"], na["bounded"])
    x = _proj_res(o, na["wo"], x, mods0, 2)
    x = yield from _moe_layer(x, mods0, moe[0])
    cosf, sinf = _rope_tables(S)
    q, k, vt = _mla_proj(x, mods1, mla["g"], mla["wd"], mla["gcq"], mla["gckv"], mla["wuq"],
                         mla["wukv"], mla["wvt"], mla["pq"], mla["pk"], mla["gq"], mla["gkn"],
                         mla["gkpe"], cosf, sinf)
    o = _flash(q, k, vt, mla["bounded"])
    x = _proj_res(o, mla["wo"], x, mods1, 2)
    x = yield from _moe_layer(x, mods1, moe[1])
    return x


def _interleave(generators):
    results = [None] * len(generators)
    live = list(range(len(generators)))
    while live:
        for i in list(live):
            try:
                next(generators[i])
            except StopIteration as done:
                results[i] = done.value
                live.remove(i)
    return results


def kernel(x_prompt, x_sample, c_prompt, c_sample, g_norm1, g_norm2, w_ada, b_ada, na_w_qkv, na_g_q, na_g_k, na_rpb, na_w_o, mla_w_down, mla_g_cq, mla_g_ckv, mla_w_uq, mla_w_ukv, mla_g_q, mla_g_k, mla_w_o, w_router, b_router, w_gate, w_up, w_down, ws_gate, ws_up, ws_down):
    na, mla, moe = _prepare(g_norm1, g_norm2, na_w_qkv, na_g_q, na_g_k, na_rpb, na_w_o,
                            mla_w_down, mla_g_cq, mla_g_ckv, mla_w_uq, mla_w_ukv, mla_g_q, mla_g_k,
                            mla_w_o, w_router, b_router, w_gate, w_up, w_down,
                            ws_gate, ws_up, ws_down)
    bp, bs = c_prompt.shape[0], c_sample.shape[0]
    c_all = jnp.concatenate([c_prompt, c_sample], axis=0)
    c_all = jnp.pad(c_all, ((0, (-c_all.shape[0]) % 8), (0, 0)))
    mods = []
    for i in range(w_ada.shape[0]):
        m = _mods(c_all, w_ada[i], b_ada[i]).reshape(c_all.shape[0], 6, D_MODEL)
        mods.append(jnp.pad(m, ((0, 0), (0, 2), (0, 0))))
    y_prompt, y_sample = _interleave([
        _trunk(x_prompt, mods[0][:bp], mods[1][:bp], na, mla, moe),
        _trunk(x_sample, mods[0][bp:bp + bs], mods[1][bp:bp + bs], na, mla, moe)])
    return (y_prompt, y_sample)
```

```python
import functools
import math

import jax
import jax.numpy as jnp
import numpy as np
from jax import lax
from jax.experimental import pallas as pl
from jax.experimental.pallas import tpu as pltpu
from jax.experimental.pallas import tpu_sc as plsc

F32 = jnp.float32
BF16 = jnp.bfloat16

D_MODEL = 1024
GRID_W = 64
NA_HEADS = 16
NA_HEAD_DIM = 64
NA_WIN_H = 8
NA_WIN_W = 16
NA_Q_ROWS = 4
NA_K_ROWS = 3 * NA_Q_ROWS
MLA_HEADS = 16
MLA_Q_RANK = 384
MLA_KV_RANK = 256
MLA_NOPE = 64
MLA_ROPE = 32
MLA_V = 64
MLA_QK = MLA_NOPE + MLA_ROPE
MLA_SLOT = 128
ROPE_THETA = 10000.0
N_EXPERTS = 64
TOP_K = 6
N_GROUPS = 8
TOPK_GROUPS = 4
EXPERT_FF = 256
ROUTED_SCALE = 2.5
EXPERT_TILE = 512
SC_CORES = 2
SC_SUBCORES = 16
SC_WORKERS = SC_CORES * SC_SUBCORES
SC_GATHER_ROWS = 64
HIGH_HALF = -65536
EPS = 1e-6
NEG = -1e30
LOG2E = math.log2(math.e)
FLASH_LOGIT_BOUND = 60.0

VMEM_LIMIT = 56 * 1024 * 1024
NT_DIMS = (((1,), (1,)), ((), ()))


def _params(*sem):
    return pltpu.CompilerParams(dimension_semantics=sem, vmem_limit_bytes=VMEM_LIMIT)


def _sigmoid(x):
    return 1.0 / (1.0 + jnp.exp(-x))


def _modnorm(x, g, sc, sh):
    ms = jnp.mean(x * x, axis=-1, keepdims=True)
    return (x * lax.rsqrt(ms + EPS) * g) * (1.0 + sc) + sh


def _rmsnorm_rows(x, g):
    ms = jnp.mean(x * x, axis=-1, keepdims=True)
    return x * lax.rsqrt(ms + EPS) * g


def _segnorm(y, p, g):
    ms = jnp.dot((y * y).astype(BF16), p, preferred_element_type=F32)
    return y * lax.rsqrt(ms + EPS) * g


def _mods_kernel(c_ref, w_ref, b_ref, o_ref):
    c = c_ref[...]
    a = c * _sigmoid(c)
    o_ref[...] = jnp.dot(a, w_ref[...], preferred_element_type=F32,
                         precision=lax.Precision.HIGHEST) + b_ref[...]


def _mods(c_all, w, b):
    rows = c_all.shape[0]
    n = w.shape[1]
    tn = 1024
    return pl.pallas_call(
        _mods_kernel,
        out_shape=jax.ShapeDtypeStruct((rows, n), F32),
        grid=(n // tn,),
        in_specs=[pl.BlockSpec((rows, D_MODEL), lambda j: (0, 0)),
                  pl.BlockSpec((D_MODEL, tn), lambda j: (0, j)),
                  pl.BlockSpec((1, tn), lambda j: (0, j))],
        out_specs=pl.BlockSpec((rows, tn), lambda j: (0, j)),
        compiler_params=_params("arbitrary"), name="adaln_mods",
    )(c_all, w, b.reshape(1, n))


def _na_qkv_kernel(x_ref, mods_ref, g_ref, w_ref, p_ref, gq_ref, gk_ref, q_ref, k_ref, v_ref):
    hn = _modnorm(x_ref[...], g_ref[...], mods_ref[1:2, :], mods_ref[0:1, :]).astype(BF16)
    for part, o_ref, gg in ((0, q_ref, gq_ref), (1, k_ref, gk_ref), (2, v_ref, None)):
        for t in range(D_MODEL // 256):
            lo = t * 256
            y = jnp.dot(hn, w_ref[:, part * D_MODEL + lo:part * D_MODEL + lo + 256],
                        preferred_element_type=F32)
            if gg is not None:
                y = _segnorm(y, p_ref[...], gg[:, lo:lo + 256])
            o_ref[:, lo:lo + 256] = y.astype(BF16)


def _na_qkv(x, mods, g, w, p, gq, gk, tm=1024):
    B, S, _ = x.shape
    tok = pl.BlockSpec((None, tm, D_MODEL), lambda b, i: (b, i, 0))
    full = lambda a: pl.BlockSpec(a.shape, lambda b, i: (0,) * a.ndim)
    out = jax.ShapeDtypeStruct((B, S, D_MODEL), BF16)
    return pl.pallas_call(
        _na_qkv_kernel,
        out_shape=(out, out, out),
        grid=(B, S // tm),
        in_specs=[tok, pl.BlockSpec((None, 8, D_MODEL), lambda b, i: (b, 0, 0)),
                  full(g), full(w), full(p), full(gq), full(gk)],
        out_specs=(tok, tok, tok),
        compiler_params=_params("parallel", "parallel"), name="na_qkv",
    )(x, mods, g, w, p, gq, gk)


def _na_attn_kernel(q_ref, k0_ref, k1_ref, k2_ref, v0_ref, v1_ref, v2_ref, bias_ref, o_ref, *, bounded):
    lane = lax.broadcasted_iota(jnp.int32, (1, 128), 1)
    first = lane < NA_HEAD_DIM
    for hp in range(NA_HEADS // 2):
        cols = slice(hp * 128, (hp + 1) * 128)
        q = q_ref[:, cols]
        k = jnp.concatenate([k0_ref[:, cols], k1_ref[:, cols], k2_ref[:, cols]], axis=0)
        v = jnp.concatenate([v0_ref[:, cols], v1_ref[:, cols], v2_ref[:, cols]], axis=0)
        if bounded:
            v = jnp.concatenate([v, jnp.ones_like(v)], axis=1)
        outs = []
        for sub in range(2):
            keep = first if sub == 0 else jnp.logical_not(first)
            qm = jnp.where(keep, q, jnp.zeros_like(q))
            s = lax.dot_general(qm, k, NT_DIMS, preferred_element_type=F32)
            s = s + bias_ref[2 * hp + sub].astype(F32)
            if bounded:
                o = jnp.dot(jnp.exp2(s).astype(BF16), v, preferred_element_type=F32)
                outs.append(o[:, :128] / o[:, 128:129])
            else:
                e = jnp.exp2(s - jnp.max(s, axis=-1, keepdims=True))
                o = jnp.dot(e.astype(BF16), v, preferred_element_type=F32)
                outs.append(o / jnp.sum(e, axis=-1, keepdims=True))
        o_ref[:, cols] = jnp.where(first, outs[0], outs[1]).astype(BF16)


def _na_attn(q, k, v, bias, bounded):
    return lax.cond(bounded, functools.partial(_na_attn_call, bounded=True),
                    functools.partial(_na_attn_call, bounded=False), q, k, v, bias)


def _na_logit_bound(g_q, g_k, rpb):
    qk = NA_HEAD_DIM * jnp.max(jnp.abs(g_q)) * jnp.max(jnp.abs(g_k)) * NA_HEAD_DIM ** -0.5
    return 1.05 * (qk + jnp.max(jnp.abs(rpb))) * LOG2E


def _na_attn_call(q, k, v, bias, *, bounded):
    B, S, _ = q.shape
    tq = NA_Q_ROWS * GRID_W
    ng = S // tq
    assert ng >= 3

    def kv_spec(j):
        return pl.BlockSpec((None, tq, D_MODEL),
                            lambda b, g: (b, jnp.clip(g - 1, 0, ng - 3) + j, 0))

    def bias_map(b, g):
        return (0, jnp.where(g == 0, 0, jnp.where(g == ng - 1, 2, 1)), 0)

    qspec = pl.BlockSpec((None, tq, D_MODEL), lambda b, g: (b, g, 0))
    return pl.pallas_call(
        functools.partial(_na_attn_kernel, bounded=bounded),
        out_shape=jax.ShapeDtypeStruct((B, S, D_MODEL), BF16),
        grid=(B, ng),
        in_specs=[qspec, kv_spec(0), kv_spec(1), kv_spec(2), kv_spec(0), kv_spec(1), kv_spec(2),
                  pl.BlockSpec((NA_HEADS, tq, 3 * tq), bias_map)],
        out_specs=qspec,
        compiler_params=_params("parallel", "arbitrary"),
        name="na_attn_bounded" if bounded else "na_attn",
    )(q, k, k, k, v, v, v, bias)


def _na_bias_table(rpb):
    rows = NA_K_ROWS
    r = np.arange(rows)
    c = np.arange(GRID_W)
    rs = np.clip(r - NA_WIN_H // 2, 0, rows - NA_WIN_H)
    cs = np.clip(c - NA_WIN_W // 2, 0, GRID_W - NA_WIN_W)
    dr = r[None, :] - r[:, None]
    dc = c[None, :] - c[:, None]
    ok_r = (r[None, :] >= rs[:, None]) & (r[None, :] < rs[:, None] + NA_WIN_H)
    ok_c = (c[None, :] >= cs[:, None]) & (c[None, :] < cs[:, None] + NA_WIN_W)
    ri = np.clip(dr + NA_WIN_H - 1, 0, 2 * NA_WIN_H - 2)
    ci = np.clip(dc + NA_WIN_W - 1, 0, 2 * NA_WIN_W - 2)
    ok = ok_r[:, None, :, None] & ok_c[None, :, None, :]
    sel_r = jnp.asarray(ri[:, :, None] == np.arange(2 * NA_WIN_H - 1), F32)
    sel_c = jnp.asarray(ci[:, :, None] == np.arange(2 * NA_WIN_W - 1), F32)
    by_row = jnp.einsum('hij,qki->hqkj', rpb, sel_r, precision=lax.Precision.HIGHEST)
    tab = jnp.einsum('hqkj,cdj->hqckd', by_row, sel_c, precision=lax.Precision.HIGHEST)
    tab = jnp.where(ok[None], tab * LOG2E, NEG)
    return tab.reshape(NA_HEADS, rows * GRID_W, rows * GRID_W).astype(BF16)


def _proj_res_kernel(o_ref, w_ref, x_ref, mods_ref, out_ref, *, gate_row):
    y = jnp.dot(o_ref[...], w_ref[...], preferred_element_type=F32)
    out_ref[...] = x_ref[...] + mods_ref[gate_row:gate_row + 1, :] * y


def _proj_res(o, w, x, mods, gate_row, tm=512):
    B, S, K = o.shape
    tok = pl.BlockSpec((None, tm, D_MODEL), lambda b, i: (b, i, 0))
    return pl.pallas_call(
        functools.partial(_proj_res_kernel, gate_row=gate_row),
        out_shape=jax.ShapeDtypeStruct((B, S, D_MODEL), F32),
        grid=(B, S // tm),
        in_specs=[pl.BlockSpec((None, tm, K), lambda b, i: (b, i, 0)),
                  pl.BlockSpec(w.shape, lambda b, i: (0, 0)),
                  tok, pl.BlockSpec((None, 8, D_MODEL), lambda b, i: (b, 0, 0))],
        out_specs=tok,
        compiler_params=_params("parallel", "parallel"), name="proj_res",
    )(o, w, x, mods)


def _rope(y, cosf, sinf, width):
    lane = lax.broadcasted_iota(jnp.int32, (1, width), 1) % MLA_SLOT
    half = MLA_ROPE // 2
    up = pltpu.roll(y, width - half, 1)
    dn = pltpu.roll(y, half, 1)
    lo = (lane >= MLA_NOPE) & (lane < MLA_NOPE + half)
    hi = (lane >= MLA_NOPE + half) & (lane < MLA_QK)
    rot = jnp.where(lo, -up, jnp.where(hi, dn, 0.0))
    return y * cosf + rot * sinf


def _mla_proj_kernel(x_ref, mods_ref, g_ref, wd_ref, gcq_ref, gckv_ref, wuq_ref, wukv_ref, wvt_ref,
                     pq_ref, pk_ref, gq_ref, gkn_ref, gkpe_ref, cos_ref, sin_ref,
                     q_ref, k_ref, vt_ref):
    hn = _modnorm(x_ref[...], g_ref[...], mods_ref[1:2, :], mods_ref[0:1, :]).astype(BF16)
    down = jnp.dot(hn, wd_ref[...], preferred_element_type=F32)
    cq = _rmsnorm_rows(down[:, :MLA_Q_RANK], gcq_ref[...]).astype(BF16)
    ckv = _rmsnorm_rows(down[:, MLA_Q_RANK:MLA_Q_RANK + MLA_KV_RANK], gckv_ref[...]).astype(BF16)
    cosf = cos_ref[...]
    sinf = sin_ref[...]
    kpe = down[:, MLA_Q_RANK + MLA_KV_RANK:]
    kpe = _segnorm(kpe, pq_ref[:MLA_SLOT, :MLA_SLOT], gkpe_ref[...])
    kpe = _rope(kpe, cosf[:, :MLA_SLOT], sinf[:, :MLA_SLOT], MLA_SLOT)
    kpe2 = jnp.concatenate([kpe, kpe], axis=1)
    sub = lax.broadcasted_iota(jnp.int32, (256, 1), 0) % MLA_SLOT
    ones_row = jnp.where(sub == MLA_V, 1.0, 0.0)
    n_tiles = MLA_HEADS * MLA_SLOT // 256
    for t in range(n_tiles):
        cols = slice(t * 256, (t + 1) * 256)
        y = jnp.dot(cq, wuq_ref[:, cols], preferred_element_type=F32)
        y = _segnorm(y, pq_ref[...], gq_ref[...])
        q_ref[:, cols] = _rope(y, cosf, sinf, 256).astype(BF16)
    for t in range(n_tiles):
        cols = slice(t * 256, (t + 1) * 256)
        y = jnp.dot(ckv, wukv_ref[:, cols], preferred_element_type=F32)
        y = _segnorm(y, pk_ref[...], gkn_ref[...])
        k_ref[:, cols] = (y + kpe2).astype(BF16)
    for t in range(n_tiles):
        rows = slice(t * 256, (t + 1) * 256)
        yt = lax.dot_general(wvt_ref[rows, :], ckv, NT_DIMS, preferred_element_type=F32)
        vt_ref[rows, :] = (yt + ones_row).astype(BF16)


def _mla_proj(x, mods, g, wd, gcq, gckv, wuq, wukv, wvt, pq, pk, gq, gkn, gkpe, cosf, sinf, tm=512):
    B, S, _ = x.shape
    wide = MLA_HEADS * MLA_SLOT
    tok = pl.BlockSpec((None, tm, D_MODEL), lambda b, i: (b, i, 0))
    full = lambda a: pl.BlockSpec(a.shape, lambda b, i: (0,) * a.ndim)
    pos = pl.BlockSpec((tm, 256), lambda b, i: (i, 0))
    out = jax.ShapeDtypeStruct((B, S, wide), BF16)
    ospec = pl.BlockSpec((None, tm, wide), lambda b, i: (b, i, 0))
    return pl.pallas_call(
        _mla_proj_kernel,
        out_shape=(out, out, jax.ShapeDtypeStruct((B, wide, S), BF16)),
        grid=(B, S // tm),
        in_specs=[tok, pl.BlockSpec((None, 8, D_MODEL), lambda b, i: (b, 0, 0)),
                  full(g), full(wd), full(gcq), full(gckv), full(wuq), full(wukv), full(wvt),
                  full(pq), full(pk), full(gq), full(gkn), full(gkpe), pos, pos],
        out_specs=(ospec, ospec, pl.BlockSpec((None, wide, tm), lambda b, i: (b, 0, i))),
        compiler_params=_params("parallel", "parallel"), name="mla_proj",
    )(x, mods, g, wd, gcq, gckv, wuq, wukv, wvt, pq, pk, gq, gkn, gkpe, cosf, sinf)


def _flash_kernel(q_ref, k_ref, vt_ref, o_ref, *, tk):
    q = q_ref[...]
    tq = q.shape[0]
    nk = k_ref.shape[0] // tk

    def body(j, carry):
        m, acc = carry
        ks = pl.multiple_of(j * tk, tk)
        k = k_ref[pl.ds(ks, tk), :]
        vt = vt_ref[:, pl.ds(ks, tk)]
        s = lax.dot_general(q, k, NT_DIMS, preferred_element_type=F32)
        m_new = jnp.maximum(m, jnp.max(s, axis=-1, keepdims=True))
        alpha = jnp.exp2(m - m_new)
        p = jnp.exp2(s - m_new)
        acc = alpha * acc + lax.dot_general(p.astype(BF16), vt, NT_DIMS, preferred_element_type=F32)
        return m_new, acc

    m0 = jnp.full((tq, 1), -jnp.inf, F32)
    acc0 = jnp.zeros((tq, MLA_SLOT), F32)
    _, acc = lax.fori_loop(0, nk, body, (m0, acc0))
    o_ref[...] = (acc / acc[:, MLA_V:MLA_V + 1]).astype(BF16)


def _flash_bounded_kernel(q_ref, k_ref, vt_ref, o_ref, acc_ref, *, tk, sub):
    nk = k_ref.shape[0] // tk
    acc_ref[...] = jnp.zeros_like(acc_ref)
    q = q_ref[...]

    def body(j, carry):
        for a in range(tk // sub):
            ks = pl.multiple_of(j * tk + a * sub, sub)
            st = lax.dot_general(k_ref[pl.ds(ks, sub), :], q, NT_DIMS, preferred_element_type=F32)
            acc_ref[...] += jnp.dot(vt_ref[:, pl.ds(ks, sub)], jnp.exp2(st).astype(BF16),
                                    preferred_element_type=F32)
        return carry

    lax.fori_loop(0, nk, body, 0, unroll=2)
    acc = acc_ref[...]
    o_ref[...] = jnp.transpose(acc / acc[MLA_V:MLA_V + 1, :]).astype(BF16)


def _flash_call(kernel_fn, q, k, vt, tq, scratch, name):
    B, S, wide = q.shape
    H = wide // MLA_SLOT
    qspec = pl.BlockSpec((None, tq, MLA_SLOT), lambda b, h, i: (b, i, h))
    kspec = pl.BlockSpec((None, S, MLA_SLOT), lambda b, h, i: (b, 0, h))
    vtspec = pl.BlockSpec((None, MLA_SLOT, S), lambda b, h, i: (b, h, 0))
    return pl.pallas_call(
        kernel_fn,
        out_shape=jax.ShapeDtypeStruct((B, S, wide), BF16),
        grid=(B, H, S // tq),
        in_specs=[qspec, kspec, vtspec],
        out_specs=qspec,
        scratch_shapes=scratch,
        compiler_params=_params("parallel", "parallel", "arbitrary"), name=name,
    )(q, k, vt)


def _flash(q, k, vt, bounded):
    fast = functools.partial(
        _flash_call, functools.partial(_flash_bounded_kernel, tk=1024, sub=512),
        tq=2048, scratch=[pltpu.VMEM((MLA_SLOT, 2048), F32)], name="mla_flash_bounded")
    general = functools.partial(
        _flash_call, functools.partial(_flash_kernel, tk=512),
        tq=512, scratch=[], name="mla_flash")
    return lax.cond(bounded, fast, general, q, k, vt)


def _mla_logit_bound(g_q, g_k):
    def seg(g):
        return MLA_NOPE * jnp.max(jnp.abs(g[:MLA_NOPE])) ** 2 + MLA_ROPE * jnp.max(jnp.abs(g[MLA_NOPE:])) ** 2
    return 1.05 * jnp.sqrt(seg(g_q) * seg(g_k)) * MLA_QK ** -0.5 * LOG2E


def _pack_halves(y):
    w = y.shape[1] // 2
    lo = pltpu.bitcast(y[:, :w].astype(BF16).astype(F32), jnp.int32)
    hi = pltpu.bitcast(y[:, w:].astype(BF16).astype(F32), jnp.int32)
    return (hi & HIGH_HALF) | lax.shift_right_logical(lo, 16)


def _unpack_halves(p):
    return pltpu.bitcast(p << 16, F32), pltpu.bitcast(p & HIGH_HALF, F32)


def _router_kernel(x_ref, mods_ref, g_ref, wrt_ref, br_ref, tri_ref,
                   hn_ref, eid_ref, wts_ref, rank_ref, cnt_ref, run_ref):
    @pl.when((pl.program_id(0) == 0) & (pl.program_id(1) == 0))
    def _():
        run_ref[...] = jnp.zeros_like(run_ref)

    hn = _modnorm(x_ref[...], g_ref[...], mods_ref[4:5, :], mods_ref[3:4, :])
    hn_ref[...] = _pack_halves(hn)
    tm = hn.shape[0]
    logits = lax.dot_general(wrt_ref[...], hn, NT_DIMS, preferred_element_type=F32,
                             precision=lax.Precision.HIGHEST)
    scores = _sigmoid(logits)
    choice = scores + br_ref[...]
    per = N_EXPERTS // N_GROUPS
    sub8 = lax.broadcasted_iota(jnp.int32, (per, tm), 0).astype(F32)
    gs = []
    for gi in range(N_GROUPS):
        cg = choice[gi * per:(gi + 1) * per, :]
        m1 = jnp.max(cg, axis=0, keepdims=True)
        i1 = jnp.min(jnp.where(cg == m1, sub8, float(per)), axis=0, keepdims=True)
        m2 = jnp.max(jnp.where(sub8 == i1, -jnp.inf, cg), axis=0, keepdims=True)
        gs.append(m1 + m2)
    gs8 = jnp.concatenate(gs, axis=0)
    rank = jnp.zeros_like(gs8)
    for gi in range(N_GROUPS):
        row = gs8[gi:gi + 1, :]
        ahead = (row > gs8) | ((row == gs8) & (sub8 > float(gi)))
        rank = rank + jnp.where(ahead, 1.0, 0.0)
    cur = jnp.concatenate(
        [jnp.where(rank[gi:gi + 1, :] < float(TOPK_GROUPS),
                   choice[gi * per:(gi + 1) * per, :], -jnp.inf)
         for gi in range(N_GROUPS)], axis=0)
    sub64 = lax.broadcasted_iota(jnp.int32, (N_EXPERTS, tm), 0).astype(F32)
    hits = []
    for _ in range(TOP_K):
        m = jnp.max(cur, axis=0, keepdims=True)
        idx = jnp.min(jnp.where(cur == m, sub64, float(N_EXPERTS)), axis=0, keepdims=True)
        hit = sub64 == idx
        hits.append((hit, idx))
        cur = jnp.where(hit, -jnp.inf, cur)
    chosen = jnp.zeros_like(cur)
    for hit, _ in hits:
        chosen = jnp.where(hit, 1.0, chosen)
    prefix = jnp.dot(chosen.astype(BF16), tri_ref[...], preferred_element_type=F32)
    rank_full = run_ref[:, 0:1] + prefix
    slot = lax.broadcasted_iota(jnp.int32, (8, tm), 0)
    eid8 = jnp.zeros((8, tm), F32)
    rank8 = jnp.zeros((8, tm), F32)
    w8 = jnp.zeros((8, tm), F32)
    for k, (hit, idx) in enumerate(hits):
        eid8 = jnp.where(slot == k, idx, eid8)
        rank8 = jnp.where(slot == k, jnp.sum(jnp.where(hit, rank_full, 0.0), axis=0, keepdims=True), rank8)
        w8 = jnp.where(slot == k, jnp.sum(jnp.where(hit, scores, 0.0), axis=0, keepdims=True), w8)
    eid_ref[...] = eid8.astype(jnp.int32)
    rank_ref[...] = rank8.astype(jnp.int32)
    wts_ref[...] = w8 / jnp.sum(w8, axis=0, keepdims=True) * ROUTED_SCALE
    run_ref[...] = run_ref[...] + jnp.sum(chosen, axis=1, keepdims=True)
    cnt_ref[...] = run_ref[...]


def _router(x, mods, g, wrt, br, tm=512):
    B, S, _ = x.shape
    tok = pl.BlockSpec((None, tm, D_MODEL), lambda b, i: (b, i, 0))
    slots = pl.BlockSpec((None, 8, tm), lambda b, i: (b, 0, i))
    tri = jnp.asarray(np.triu(np.ones((tm, tm), np.float32), 1), BF16)
    const = lambda a: pl.BlockSpec(a.shape, lambda b, i: (0, 0))
    return pl.pallas_call(
        _router_kernel,
        out_shape=(jax.ShapeDtypeStruct((B, S, D_MODEL // 2), jnp.int32),
                   jax.ShapeDtypeStruct((B, 8, S), jnp.int32),
                   jax.ShapeDtypeStruct((B, 8, S), F32),
                   jax.ShapeDtypeStruct((B, 8, S), jnp.int32),
                   jax.ShapeDtypeStruct((N_EXPERTS, 128), F32)),
        grid=(B, S // tm),
        in_specs=[tok, pl.BlockSpec((None, 8, D_MODEL), lambda b, i: (b, 0, 0)),
                  const(g), const(wrt), const(br), const(tri)],
        out_specs=(pl.BlockSpec((None, tm, D_MODEL // 2), lambda b, i: (b, i, 0)),
                   slots, slots, slots,
                   pl.BlockSpec((N_EXPERTS, 128), lambda b, i: (0, 0))),
        scratch_shapes=[pltpu.VMEM((N_EXPERTS, 128), F32)],
        compiler_params=_params("arbitrary", "arbitrary"), name="moe_router",
    )(x, mods, g, wrt, br, tri)


def _sc_gather(table, idx):
    R = idx.shape[0]
    W = table.shape[1]
    per_w = R // SC_WORKERS
    steps = per_w // SC_GATHER_ROWS
    assert per_w * SC_WORKERS == R and steps * SC_GATHER_ROWS == per_w and steps % 2 == 0
    idx3 = idx.reshape(SC_WORKERS, steps, SC_GATHER_ROWS)
    mesh = plsc.VectorSubcoreMesh(core_axis_name="c", subcore_axis_name="s")

    @functools.partial(
        pl.kernel, mesh=mesh,
        out_type=jax.ShapeDtypeStruct((R, W), jnp.int32),
        scratch_types=[pltpu.VMEM((steps, SC_GATHER_ROWS), jnp.int32),
                       pltpu.VMEM((2, SC_GATHER_ROWS, W), jnp.int32),
                       pltpu.SemaphoreType.DMA((2,)),
                       pltpu.SemaphoreType.DMA((2,))],
    )
    def gather_kernel(table_hbm, idx_hbm, out_hbm, idx_v, rows_v, gsem, wsem):
        wid = lax.axis_index("s") * SC_CORES + lax.axis_index("c")
        base = wid * per_w
        pltpu.sync_copy(idx_hbm.at[wid], idx_v)

        def gather(j, b):
            return pltpu.make_async_copy(table_hbm.at[idx_v.at[j]], rows_v.at[b], gsem.at[b])

        def write(j, b):
            return pltpu.make_async_copy(
                rows_v.at[b], out_hbm.at[pl.ds(base + j * SC_GATHER_ROWS, SC_GATHER_ROWS)], wsem.at[b])

        gather(0, 0).start()
        gather(1, 1).start()

        @pl.loop(0, steps, step=2)
        def _(j):
            for b in range(2):
                gather(j + b, b).wait()
                write(j + b, b).start()
                write(j + b, b).wait()

                @pl.when(j + b + 2 < steps)
                def _():
                    gather(j + b + 2, b).start()

    return gather_kernel(table, idx3)


def _sc_scatter(rows, idx, n_out):
    A = idx.shape[0]
    V, W = rows.shape
    per_w = A // SC_WORKERS
    steps = per_w // SC_GATHER_ROWS
    assert per_w * SC_WORKERS == A and steps * SC_GATHER_ROWS == per_w and steps % 2 == 0
    assert V % SC_GATHER_ROWS == 0
    idx3 = idx.reshape(SC_WORKERS, steps, SC_GATHER_ROWS)
    mesh = plsc.VectorSubcoreMesh(core_axis_name="c", subcore_axis_name="s")

    @functools.partial(
        pl.kernel, mesh=mesh,
        out_type=jax.ShapeDtypeStruct((n_out, W), jnp.int32),
        scratch_types=[pltpu.VMEM((steps, SC_GATHER_ROWS), jnp.int32),
                       pltpu.VMEM((2, SC_GATHER_ROWS, W), jnp.int32),
                       pltpu.SemaphoreType.DMA((2,)),
                       pltpu.SemaphoreType.DMA((2,))],
    )
    def scatter_kernel(rows_hbm, idx_hbm, out_hbm, idx_v, buf_v, rsem, wsem):
        wid = lax.axis_index("s") * SC_CORES + lax.axis_index("c")
        base = wid * per_w
        pltpu.sync_copy(idx_hbm.at[wid], idx_v)

        def read(j, b):
            src = lax.rem(base + j * SC_GATHER_ROWS, V)
            return pltpu.make_async_copy(rows_hbm.at[pl.ds(src, SC_GATHER_ROWS)], buf_v.at[b], rsem.at[b])

        def write(j, b):
            return pltpu.make_async_copy(buf_v.at[b], out_hbm.at[idx_v.at[j]], wsem.at[b])

        read(0, 0).start()
        read(1, 1).start()

        @pl.loop(0, steps, step=2)
        def _(j):
            for b in range(2):
                read(j + b, b).wait()
                write(j + b, b).start()
                write(j + b, b).wait()

                @pl.when(j + b + 2 < steps)
                def _():
                    read(j + b + 2, b).start()

    return scatter_kernel(rows, idx3)


def _dispatch_plan(eid, rank, counts, n_rows):
    eid = eid[:, :TOP_K, :]
    rank = rank[:, :TOP_K, :]
    counts = counts[:, 0].astype(jnp.int32)
    padded = (counts + EXPERT_TILE - 1) // EXPERT_TILE * EXPERT_TILE
    ends = jnp.cumsum(padded)
    offs = ends - padded
    experts = jnp.arange(N_EXPERTS, dtype=jnp.int32)
    row = rank + jnp.sum(jnp.where(eid[..., None] == experts, offs, 0), axis=-1)
    row_slotmajor = jnp.transpose(row, (1, 0, 2)).reshape(-1)
    tile_start = jnp.arange(n_rows // EXPERT_TILE, dtype=jnp.int32) * EXPERT_TILE
    tile_expert = jnp.minimum(jnp.sum(tile_start[:, None] >= ends[None, :], axis=1), N_EXPERTS - 1)
    live_end = jnp.sum(jnp.where(tile_expert[:, None] == experts, offs + counts, 0), axis=1)
    tile_live = jnp.clip(live_end - tile_start, 0, EXPERT_TILE)
    n_tiles = (ends[-1] // EXPERT_TILE).reshape(1)
    return (row_slotmajor, tile_expert.astype(jnp.int32), tile_live.astype(jnp.int32),
            n_tiles.astype(jnp.int32))


def _experts_kernel(te_ref, tl_ref, nt_ref, xs_ref, wg_ref, wu_ref, wd_ref, y_ref):
    i = pl.program_id(0)

    @pl.when(i < nt_ref[0])
    def _():
        live = lax.broadcasted_iota(jnp.int32, (EXPERT_TILE, 1), 0) < tl_ref[i]
        lo, hi = _unpack_halves(jnp.where(live, xs_ref[...], 0))
        x = jnp.concatenate([lo.astype(BF16), hi.astype(BF16)], axis=1)
        a = jnp.dot(x, wg_ref[...].astype(BF16), preferred_element_type=F32)
        u = jnp.dot(x, wu_ref[...].astype(BF16), preferred_element_type=F32)
        h = ((a * _sigmoid(a)) * u).astype(BF16)
        y_ref[...] = _pack_halves(jnp.dot(h, wd_ref[...].astype(BF16), preferred_element_type=F32))

    @pl.when(i >= nt_ref[0])
    def _():
        y_ref[...] = jnp.zeros_like(y_ref)


def _experts(xs, tile_expert, tile_live, n_tiles, wg, wu, wd, layer):
    n_rows, half = xs.shape
    rows = pl.BlockSpec((EXPERT_TILE, half), lambda i, te, tl, nt: (i, 0))
    w_in = pl.BlockSpec((None, None, D_MODEL, EXPERT_FF), lambda i, te, tl, nt: (layer, te[i], 0, 0))
    w_out = pl.BlockSpec((None, None, EXPERT_FF, D_MODEL), lambda i, te, tl, nt: (layer, te[i], 0, 0))
    return pl.pallas_call(
        _experts_kernel,
        out_shape=jax.ShapeDtypeStruct((n_rows, half), jnp.int32),
        grid_spec=pltpu.PrefetchScalarGridSpec(
            num_scalar_prefetch=3,
            grid=(n_rows // EXPERT_TILE,),
            in_specs=[rows, w_in, w_in, w_out],
            out_specs=rows),
        compiler_params=_params("arbitrary"), name="moe_experts",
    )(tile_expert, tile_live, n_tiles, xs, wg, wu, wd)


def _combine_kernel(yg_ref, wts_ref, hn_ref, wsg_ref, wsu_ref, wsd_ref, x_ref, mods_ref, out_ref):
    lo, hi = _unpack_halves(hn_ref[...])
    hn = jnp.concatenate([lo.astype(BF16), hi.astype(BF16)], axis=1)
    a = jnp.dot(hn, wsg_ref[...], preferred_element_type=F32)
    u = jnp.dot(hn, wsu_ref[...], preferred_element_type=F32)
    shared = jnp.dot(((a * _sigmoid(a)) * u).astype(BF16), wsd_ref[...], preferred_element_type=F32)
    half = D_MODEL // 2
    acc_lo = shared[:, :half]
    acc_hi = shared[:, half:]
    wts = wts_ref[...]
    for k in range(TOP_K):
        lo, hi = _unpack_halves(yg_ref[k])
        wk = wts[:, k:k + 1]
        acc_lo = acc_lo + wk * lo
        acc_hi = acc_hi + wk * hi
    g2 = mods_ref[5:6, :]
    out_ref[:, :half] = x_ref[:, :half] + g2[:, :half] * acc_lo
    out_ref[:, half:] = x_ref[:, half:] + g2[:, half:] * acc_hi


def _combine(yg, wts, hn, wsg, wsu, wsd, x, mods, tm=512):
    B, S, _ = x.shape
    half = D_MODEL // 2
    tok = pl.BlockSpec((None, tm, D_MODEL), lambda b, i: (b, i, 0))
    const = lambda a: pl.BlockSpec(a.shape, lambda b, i: (0,) * a.ndim)
    return pl.pallas_call(
        _combine_kernel,
        out_shape=jax.ShapeDtypeStruct((B, S, D_MODEL), F32),
        grid=(B, S // tm),
        in_specs=[pl.BlockSpec((TOP_K, None, tm, half), lambda b, i: (0, b, i, 0)),
                  pl.BlockSpec((None, tm, 8), lambda b, i: (b, i, 0)),
                  pl.BlockSpec((None, tm, half), lambda b, i: (b, i, 0)),
                  const(wsg), const(wsu), const(wsd),
                  tok, pl.BlockSpec((None, 8, D_MODEL), lambda b, i: (b, 0, 0))],
        out_specs=tok,
        compiler_params=_params("parallel", "parallel"), name="moe_combine",
    )(yg, wts, hn, wsg, wsu, wsd, x, mods)


def _block_diag_mean(segments, width):
    p = np.zeros((width, width), np.float32)
    pos = 0
    while pos < width:
        for length, live in segments:
            if live:
                p[pos:pos + length, pos:pos + length] = 1.0 / length
            pos += length
    return jnp.asarray(p, BF16)


def _slot_layout(w, n_heads, per_head, take, slot=MLA_SLOT):
    k = w.shape[0]
    wh = w.reshape(k, n_heads, per_head)[:, :, take]
    wh = jnp.pad(wh, ((0, 0), (0, 0), (0, slot - wh.shape[-1])))
    return wh.reshape(k, n_heads * slot)


def _slot_vector(pieces):
    v = jnp.concatenate(pieces)
    v = jnp.pad(v, (0, MLA_SLOT - v.shape[0]))
    return jnp.tile(v, 2).reshape(1, 2 * MLA_SLOT)


def _rope_tables(S):
    half = MLA_ROPE // 2
    inv = 1.0 / (ROPE_THETA ** (jnp.arange(half, dtype=F32) / half))
    ang = jnp.arange(S, dtype=F32)[:, None] * inv[None, :]
    cos, sin = jnp.cos(ang), jnp.sin(ang)
    ones = jnp.ones((S, MLA_NOPE), F32)
    pad = jnp.zeros((S, MLA_SLOT - MLA_QK), F32)
    cosf = jnp.concatenate([ones, cos, cos, pad], axis=1)
    sinf = jnp.concatenate([0.0 * ones, sin, sin, pad], axis=1)
    return jnp.tile(cosf, (1, 2)), jnp.tile(sinf, (1, 2))


def _prepare(g_norm1, g_norm2, na_w_qkv, na_g_q, na_g_k, na_rpb, na_w_o,
             mla_w_down, mla_g_cq, mla_g_ckv, mla_w_uq, mla_w_ukv, mla_g_q, mla_g_k, mla_w_o,
             w_router, b_router, w_gate, w_up, w_down, ws_gate, ws_up, ws_down):
    na = dict(
        g=g_norm1[0].reshape(1, D_MODEL),
        w=na_w_qkv[0].astype(BF16),
        p=_block_diag_mean([(NA_HEAD_DIM, True)], 256),
        gq=(jnp.tile(na_g_q[0], NA_HEADS) * (NA_HEAD_DIM ** -0.5 * LOG2E)).reshape(1, D_MODEL),
        gk=jnp.tile(na_g_k[0], NA_HEADS).reshape(1, D_MODEL),
        bias=_na_bias_table(na_rpb[0]),
        bounded=_na_logit_bound(na_g_q[0], na_g_k[0], na_rpb[0]) < FLASH_LOGIT_BOUND,
        wo=na_w_o[0].astype(BF16),
    )
    wdn = mla_w_down[0]
    kpe_cols = jnp.pad(wdn[:, MLA_Q_RANK + MLA_KV_RANK:],
                       ((0, 0), (MLA_NOPE, MLA_SLOT - MLA_QK)))
    nope = np.arange(MLA_NOPE)
    wuq = mla_w_uq[0]
    wukv = mla_w_ukv[0]
    wo = mla_w_o[0].reshape(MLA_HEADS, MLA_V, D_MODEL)
    wo = jnp.pad(wo, ((0, 0), (0, MLA_SLOT - MLA_V), (0, 0))).reshape(MLA_HEADS * MLA_SLOT, D_MODEL)
    gq = mla_g_q[0] * (MLA_QK ** -0.5 * LOG2E)
    gk = mla_g_k[0]
    zeros_nope = jnp.zeros((MLA_NOPE,), F32)
    mla = dict(
        g=g_norm1[1].reshape(1, D_MODEL),
        wd=jnp.concatenate([wdn[:, :MLA_Q_RANK + MLA_KV_RANK], kpe_cols], axis=1).astype(BF16),
        gcq=mla_g_cq[0].reshape(1, MLA_Q_RANK),
        gckv=mla_g_ckv[0].reshape(1, MLA_KV_RANK),
        wuq=_slot_layout(wuq, MLA_HEADS, MLA_QK, np.arange(MLA_QK)).astype(BF16),
        wukv=_slot_layout(wukv, MLA_HEADS, MLA_NOPE + MLA_V, nope).astype(BF16),
        wvt=_slot_layout(wukv, MLA_HEADS, MLA_NOPE + MLA_V, MLA_NOPE + np.arange(MLA_V)).T.astype(BF16),
        pq=_block_diag_mean([(MLA_NOPE, True), (MLA_ROPE, True), (MLA_SLOT - MLA_QK, False)], 256),
        pk=_block_diag_mean([(MLA_NOPE, True), (MLA_SLOT - MLA_NOPE, False)], 256),
        gq=_slot_vector([gq]),
        gkn=_slot_vector([gk[:MLA_NOPE]]),
        gkpe=_slot_vector([zeros_nope, gk[MLA_NOPE:]])[:, :MLA_SLOT],
        wo=wo.astype(BF16),
        bounded=_mla_logit_bound(mla_g_q[0], mla_g_k[0]) < FLASH_LOGIT_BOUND,
    )
    moe = []
    for i in range(w_router.shape[0]):
        moe.append(dict(
            g=g_norm2[i].reshape(1, D_MODEL),
            wrt=w_router[i].T,
            br=b_router[i].reshape(N_EXPERTS, 1),
            wg=w_gate, wu=w_up, wd=w_down, layer=i,
            wsg=ws_gate[i].astype(BF16), wsu=ws_up[i].astype(BF16), wsd=ws_down[i].astype(BF16),
        ))
    return na, mla, moe


def _moe_layer(x, mods, p):
    B, S, _ = x.shape
    half = D_MODEL // 2
    hn, eid, wts, rank, counts = _router(x, mods, p["g"], p["wrt"], p["br"])
    n_rows = B * S * TOP_K + N_EXPERTS * EXPERT_TILE
    rows, tile_expert, tile_live, n_tiles = _dispatch_plan(eid, rank, counts, n_rows)
    xs = _sc_scatter(hn.reshape(B * S, half), rows, n_rows)
    yield
    ys = _experts(xs, tile_expert, tile_live, n_tiles, p["wg"], p["wu"], p["wd"], p["layer"])
    yg = _sc_gather(ys, rows).reshape(TOP_K, B, S, half)
    yield
    return _combine(yg, jnp.swapaxes(wts, 1, 2), hn, p["wsg"], p["wsu"], p["wsd"], x, mods)


def _trunk(x, mods0, mods1, na, mla, moe):
    S = x.shape[1]
    q, k, v = _na_qkv(x, mods0, na["g"], na["w"], na["p"], na["gq"], na["gk"])
    o = _na_attn(q, k, v, na["bias"], na["bounded"])
    x = _proj_res(o, na["wo"], x, mods0, 2)
    x = yield from _moe_layer(x, mods0, moe[0])
    cosf, sinf = _rope_tables(S)
    q, k, vt = _mla_proj(x, mods1, mla["g"], mla["wd"], mla["gcq"], mla["gckv"], mla["wuq"],
                         mla["wukv"], mla["wvt"], mla["pq"], mla["pk"], mla["gq"], mla["gkn"],
                         mla["gkpe"], cosf, sinf)
    o = _flash(q, k, vt, mla["bounded"])
    x = _proj_res(o, mla["wo"], x, mods1, 2)
    x = yield from _moe_layer(x, mods1, moe[1])
    return x


def _interleave(generators):
    results = [None] * len(generators)
    live = list(range(len(generators)))
    while live:
        for i in list(live):
            try:
                next(generators[i])
            except StopIteration as done:
                results[i] = done.value
                live.remove(i)
    return results


def kernel(x_prompt, x_sample, c_prompt, c_sample, g_norm1, g_norm2, w_ada, b_ada, na_w_qkv, na_g_q, na_g_k, na_rpb, na_w_o, mla_w_down, mla_g_cq, mla_g_ckv, mla_w_uq, mla_w_ukv, mla_g_q, mla_g_k, mla_w_o, w_router, b_router, w_gate, w_up, w_down, ws_gate, ws_up, ws_down):
    na, mla, moe = _prepare(g_norm1, g_norm2, na_w_qkv, na_g_q, na_g_k, na_rpb, na_w_o,
                            mla_w_down, mla_g_cq, mla_g_ckv, mla_w_uq, mla_w_ukv, mla_g_q, mla_g_k,
                            mla_w_o, w_router, b_router, w_gate, w_up, w_down,
                            ws_gate, ws_up, ws_down)
    bp, bs = c_prompt.shape[0], c_sample.shape[0]
    c_all = jnp.concatenate([c_prompt, c_sample], axis=0)
    c_all = jnp.pad(c_all, ((0, (-c_all.shape[0]) % 8), (0, 0)))
    mods = []
    for i in range(w_ada.shape[0]):
        m = _mods(c_all, w_ada[i], b_ada[i]).reshape(c_all.shape[0], 6, D_MODEL)
        mods.append(jnp.pad(m, ((0, 0), (0, 2), (0, 0))))
    y_prompt, y_sample = _interleave([
        _trunk(x_prompt, mods[0][:bp], mods[1][:bp], na, mla, moe),
        _trunk(x_sample, mods[0][bp:bp + bs], mods[1][bp:bp + bs], na, mla, moe)])
    return (y_prompt, y_sample)
```

```python
import functools
import math

import jax
import jax.numpy as jnp
import numpy as np
from jax import lax
from jax.experimental import pallas as pl
from jax.experimental.pallas import tpu as pltpu
from jax.experimental.pallas import tpu_sc as plsc

F32 = jnp.float32
BF16 = jnp.bfloat16

D_MODEL = 1024
GRID_W = 64
NA_HEADS = 16
NA_HEAD_DIM = 64
NA_WIN_H = 8
NA_WIN_W = 16
NA_Q_ROWS = 4
NA_K_ROWS = 3 * NA_Q_ROWS
MLA_HEADS = 16
MLA_Q_RANK = 384
MLA_KV_RANK = 256
MLA_NOPE = 64
MLA_ROPE = 32
MLA_V = 64
MLA_QK = MLA_NOPE + MLA_ROPE
MLA_SLOT = 128
ROPE_THETA = 10000.0
N_EXPERTS = 64
TOP_K = 6
N_GROUPS = 8
TOPK_GROUPS = 4
EXPERT_FF = 256
ROUTED_SCALE = 2.5
EXPERT_TILE = 512
SC_CORES = 2
SC_SUBCORES = 16
SC_WORKERS = SC_CORES * SC_SUBCORES
SC_GATHER_ROWS = 64
HIGH_HALF = -65536
EPS = 1e-6
NEG = -1e30
LOG2E = math.log2(math.e)
FLASH_LOGIT_BOUND = 60.0

VMEM_LIMIT = 56 * 1024 * 1024
NT_DIMS = (((1,), (1,)), ((), ()))


def _params(*sem):
    return pltpu.CompilerParams(dimension_semantics=sem, vmem_limit_bytes=VMEM_LIMIT)


def _sigmoid(x):
    return 1.0 / (1.0 + jnp.exp(-x))


def _modnorm(x, g, sc, sh):
    ms = jnp.mean(x * x, axis=-1, keepdims=True)
    return (x * lax.rsqrt(ms + EPS) * g) * (1.0 + sc) + sh


def _rmsnorm_rows(x, g):
    ms = jnp.mean(x * x, axis=-1, keepdims=True)
    return x * lax.rsqrt(ms + EPS) * g


def _segnorm(y, p, g):
    ms = jnp.dot((y * y).astype(BF16), p, preferred_element_type=F32)
    return y * lax.rsqrt(ms + EPS) * g


def _mods_kernel(c_ref, w_ref, b_ref, o_ref):
    c = c_ref[...]
    a = c * _sigmoid(c)
    o_ref[...] = jnp.dot(a, w_ref[...], preferred_element_type=F32,
                         precision=lax.Precision.HIGHEST) + b_ref[...]


def _mods(c_all, w, b):
    rows = c_all.shape[0]
    n = w.shape[1]
    tn = 1024
    return pl.pallas_call(
        _mods_kernel,
        out_shape=jax.ShapeDtypeStruct((rows, n), F32),
        grid=(n // tn,),
        in_specs=[pl.BlockSpec((rows, D_MODEL), lambda j: (0, 0)),
                  pl.BlockSpec((D_MODEL, tn), lambda j: (0, j)),
                  pl.BlockSpec((1, tn), lambda j: (0, j))],
        out_specs=pl.BlockSpec((rows, tn), lambda j: (0, j)),
        compiler_params=_params("arbitrary"), name="adaln_mods",
    )(c_all, w, b.reshape(1, n))


def _na_qkv_kernel(x_ref, mods_ref, g_ref, w_ref, p_ref, gq_ref, gk_ref, q_ref, k_ref, v_ref):
    hn = _modnorm(x_ref[...], g_ref[...], mods_ref[1:2, :], mods_ref[0:1, :]).astype(BF16)
    for part, o_ref, gg in ((0, q_ref, gq_ref), (1, k_ref, gk_ref), (2, v_ref, None)):
        for t in range(D_MODEL // 256):
            lo = t * 256
            y = jnp.dot(hn, w_ref[:, part * D_MODEL + lo:part * D_MODEL + lo + 256],
                        preferred_element_type=F32)
            if gg is not None:
                y = _segnorm(y, p_ref[...], gg[:, lo:lo + 256])
            o_ref[:, lo:lo + 256] = y.astype(BF16)


def _na_qkv(x, mods, g, w, p, gq, gk, tm=1024):
    B, S, _ = x.shape
    tok = pl.BlockSpec((None, tm, D_MODEL), lambda b, i: (b, i, 0))
    full = lambda a: pl.BlockSpec(a.shape, lambda b, i: (0,) * a.ndim)
    out = jax.ShapeDtypeStruct((B, S, D_MODEL), BF16)
    return pl.pallas_call(
        _na_qkv_kernel,
        out_shape=(out, out, out),
        grid=(B, S // tm),
        in_specs=[tok, pl.BlockSpec((None, 8, D_MODEL), lambda b, i: (b, 0, 0)),
                  full(g), full(w), full(p), full(gq), full(gk)],
        out_specs=(tok, tok, tok),
        compiler_params=_params("parallel", "parallel"), name="na_qkv",
    )(x, mods, g, w, p, gq, gk)


def _na_attn_kernel(q_ref, k0_ref, k1_ref, k2_ref, v0_ref, v1_ref, v2_ref, bias_ref, o_ref, *, bounded):
    lane = lax.broadcasted_iota(jnp.int32, (1, 128), 1)
    first = lane < NA_HEAD_DIM
    for hp in range(NA_HEADS // 2):
        cols = slice(hp * 128, (hp + 1) * 128)
        q = q_ref[:, cols]
        k = jnp.concatenate([k0_ref[:, cols], k1_ref[:, cols], k2_ref[:, cols]], axis=0)
        v = jnp.concatenate([v0_ref[:, cols], v1_ref[:, cols], v2_ref[:, cols]], axis=0)
        if bounded:
            v = jnp.concatenate([v, jnp.ones_like(v)], axis=1)
        outs = []
        for sub in range(2):
            keep = first if sub == 0 else jnp.logical_not(first)
            qm = jnp.where(keep, q, jnp.zeros_like(q))
            s = lax.dot_general(qm, k, NT_DIMS, preferred_element_type=F32)
            s = s + bias_ref[2 * hp + sub].astype(F32)
            if bounded:
                o = jnp.dot(jnp.exp2(s).astype(BF16), v, preferred_element_type=F32)
                outs.append(o[:, :128] / o[:, 128:129])
            else:
                e = jnp.exp2(s - jnp.max(s, axis=-1, keepdims=True))
                o = jnp.dot(e.astype(BF16), v, preferred_element_type=F32)
                outs.append(o / jnp.sum(e, axis=-1, keepdims=True))
        o_ref[:, cols] = jnp.where(first, outs[0], outs[1]).astype(BF16)


def _na_attn(q, k, v, bias, bounded):
    return lax.cond(bounded, functools.partial(_na_attn_call, bounded=True),
                    functools.partial(_na_attn_call, bounded=False), q, k, v, bias)


def _na_logit_bound(g_q, g_k, rpb):
    qk = NA_HEAD_DIM * jnp.max(jnp.abs(g_q)) * jnp.max(jnp.abs(g_k)) * NA_HEAD_DIM ** -0.5
    return 1.05 * (qk + jnp.max(jnp.abs(rpb))) * LOG2E


def _na_attn_call(q, k, v, bias, *, bounded):
    B, S, _ = q.shape
    tq = NA_Q_ROWS * GRID_W
    ng = S // tq
    assert ng >= 3

    def kv_spec(j):
        return pl.BlockSpec((None, tq, D_MODEL),
                            lambda b, g: (b, jnp.clip(g - 1, 0, ng - 3) + j, 0))

    def bias_map(b, g):
        return (0, jnp.where(g == 0, 0, jnp.where(g == ng - 1, 2, 1)), 0)

    qspec = pl.BlockSpec((None, tq, D_MODEL), lambda b, g: (b, g, 0))
    return pl.pallas_call(
        functools.partial(_na_attn_kernel, bounded=bounded),
        out_shape=jax.ShapeDtypeStruct((B, S, D_MODEL), BF16),
        grid=(B, ng),
        in_specs=[qspec, kv_spec(0), kv_spec(1), kv_spec(2), kv_spec(0), kv_spec(1), kv_spec(2),
                  pl.BlockSpec((NA_HEADS, tq, 3 * tq), bias_map)],
        out_specs=qspec,
        compiler_params=_params("parallel", "arbitrary"),
        name="na_attn_bounded" if bounded else "na_attn",
    )(q, k, k, k, v, v, v, bias)


def _na_bias_table(rpb):
    rows = NA_K_ROWS
    r = np.arange(rows)
    c = np.arange(GRID_W)
    rs = np.clip(r - NA_WIN_H // 2, 0, rows - NA_WIN_H)
    cs = np.clip(c - NA_WIN_W // 2, 0, GRID_W - NA_WIN_W)
    dr = r[None, :] - r[:, None]
    dc = c[None, :] - c[:, None]
    ok_r = (r[None, :] >= rs[:, None]) & (r[None, :] < rs[:, None] + NA_WIN_H)
    ok_c = (c[None, :] >= cs[:, None]) & (c[None, :] < cs[:, None] + NA_WIN_W)
    ri = np.clip(dr + NA_WIN_H - 1, 0, 2 * NA_WIN_H - 2)
    ci = np.clip(dc + NA_WIN_W - 1, 0, 2 * NA_WIN_W - 2)
    ok = ok_r[:, None, :, None] & ok_c[None, :, None, :]
    sel_r = jnp.asarray(ri[:, :, None] == np.arange(2 * NA_WIN_H - 1), F32)
    sel_c = jnp.asarray(ci[:, :, None] == np.arange(2 * NA_WIN_W - 1), F32)
    by_row = jnp.einsum('hij,qki->hqkj', rpb, sel_r, precision=lax.Precision.HIGHEST)
    tab = jnp.einsum('hqkj,cdj->hqckd', by_row, sel_c, precision=lax.Precision.HIGHEST)
    tab = jnp.where(ok[None], tab * LOG2E, NEG)
    return tab.reshape(NA_HEADS, rows * GRID_W, rows * GRID_W).astype(BF16)


def _proj_res_kernel(o_ref, w_ref, x_ref, mods_ref, out_ref, *, gate_row):
    y = jnp.dot(o_ref[...], w_ref[...], preferred_element_type=F32)
    out_ref[...] = x_ref[...] + mods_ref[gate_row:gate_row + 1, :] * y


def _proj_res(o, w, x, mods, gate_row, tm=512):
    B, S, K = o.shape
    tok = pl.BlockSpec((None, tm, D_MODEL), lambda b, i: (b, i, 0))
    return pl.pallas_call(
        functools.partial(_proj_res_kernel, gate_row=gate_row),
        out_shape=jax.ShapeDtypeStruct((B, S, D_MODEL), F32),
        grid=(B, S // tm),
        in_specs=[pl.BlockSpec((None, tm, K), lambda b, i: (b, i, 0)),
                  pl.BlockSpec(w.shape, lambda b, i: (0, 0)),
                  tok, pl.BlockSpec((None, 8, D_MODEL), lambda b, i: (b, 0, 0))],
        out_specs=tok,
        compiler_params=_params("parallel", "parallel"), name="proj_res",
    )(o, w, x, mods)


def _rope(y, cosf, sinf, width):
    lane = lax.broadcasted_iota(jnp.int32, (1, width), 1) % MLA_SLOT
    half = MLA_ROPE // 2
    up = pltpu.roll(y, width - half, 1)
    dn = pltpu.roll(y, half, 1)
    lo = (lane >= MLA_NOPE) & (lane < MLA_NOPE + half)
    hi = (lane >= MLA_NOPE + half) & (lane < MLA_QK)
    rot = jnp.where(lo, -up, jnp.where(hi, dn, 0.0))
    return y * cosf + rot * sinf


def _mla_proj_kernel(x_ref, mods_ref, g_ref, wd_ref, gcq_ref, gckv_ref, wuq_ref, wukv_ref, wvt_ref,
                     pq_ref, pk_ref, gq_ref, gkn_ref, gkpe_ref, cos_ref, sin_ref,
                     q_ref, k_ref, vt_ref):
    hn = _modnorm(x_ref[...], g_ref[...], mods_ref[1:2, :], mods_ref[0:1, :]).astype(BF16)
    down = jnp.dot(hn, wd_ref[...], preferred_element_type=F32)
    cq = _rmsnorm_rows(down[:, :MLA_Q_RANK], gcq_ref[...]).astype(BF16)
    ckv = _rmsnorm_rows(down[:, MLA_Q_RANK:MLA_Q_RANK + MLA_KV_RANK], gckv_ref[...]).astype(BF16)
    cosf = cos_ref[...]
    sinf = sin_ref[...]
    kpe = down[:, MLA_Q_RANK + MLA_KV_RANK:]
    kpe = _segnorm(kpe, pq_ref[:MLA_SLOT, :MLA_SLOT], gkpe_ref[...])
    kpe = _rope(kpe, cosf[:, :MLA_SLOT], sinf[:, :MLA_SLOT], MLA_SLOT)
    kpe2 = jnp.concatenate([kpe, kpe], axis=1)
    sub = lax.broadcasted_iota(jnp.int32, (256, 1), 0) % MLA_SLOT
    ones_row = jnp.where(sub == MLA_V, 1.0, 0.0)
    n_tiles = MLA_HEADS * MLA_SLOT // 256
    for t in range(n_tiles):
        cols = slice(t * 256, (t + 1) * 256)
        y = jnp.dot(cq, wuq_ref[:, cols], preferred_element_type=F32)
        y = _segnorm(y, pq_ref[...], gq_ref[...])
        q_ref[:, cols] = _rope(y, cosf, sinf, 256).astype(BF16)
    for t in range(n_tiles):
        cols = slice(t * 256, (t + 1) * 256)
        y = jnp.dot(ckv, wukv_ref[:, cols], preferred_element_type=F32)
        y = _segnorm(y, pk_ref[...], gkn_ref[...])
        k_ref[:, cols] = (y + kpe2).astype(BF16)
    for t in range(n_tiles):
        rows = slice(t * 256, (t + 1) * 256)
        yt = lax.dot_general(wvt_ref[rows, :], ckv, NT_DIMS, preferred_element_type=F32)
        vt_ref[rows, :] = (yt + ones_row).astype(BF16)


def _mla_proj(x, mods, g, wd, gcq, gckv, wuq, wukv, wvt, pq, pk, gq, gkn, gkpe, cosf, sinf, tm=512):
    B, S, _ = x.shape
    wide = MLA_HEADS * MLA_SLOT
    tok = pl.BlockSpec((None, tm, D_MODEL), lambda b, i: (b, i, 0))
    full = lambda a: pl.BlockSpec(a.shape, lambda b, i: (0,) * a.ndim)
    pos = pl.BlockSpec((tm, 256), lambda b, i: (i, 0))
    out = jax.ShapeDtypeStruct((B, S, wide), BF16)
    ospec = pl.BlockSpec((None, tm, wide), lambda b, i: (b, i, 0))
    return pl.pallas_call(
        _mla_proj_kernel,
        out_shape=(out, out, jax.ShapeDtypeStruct((B, wide, S), BF16)),
        grid=(B, S // tm),
        in_specs=[tok, pl.BlockSpec((None, 8, D_MODEL), lambda b, i: (b, 0, 0)),
                  full(g), full(wd), full(gcq), full(gckv), full(wuq), full(wukv), full(wvt),
                  full(pq), full(pk), full(gq), full(gkn), full(gkpe), pos, pos],
        out_specs=(ospec, ospec, pl.BlockSpec((None, wide, tm), lambda b, i: (b, 0, i))),
        compiler_params=_params("parallel", "parallel"), name="mla_proj",
    )(x, mods, g, wd, gcq, gckv, wuq, wukv, wvt, pq, pk, gq, gkn, gkpe, cosf, sinf)


def _flash_kernel(q_ref, k_ref, vt_ref, o_ref, *, tk):
    q = q_ref[...]
    tq = q.shape[0]
    nk = k_ref.shape[0] // tk

    def body(j, carry):
        m, acc = carry
        ks = pl.multiple_of(j * tk, tk)
        k = k_ref[pl.ds(ks, tk), :]
        vt = vt_ref[:, pl.ds(ks, tk)]
        s = lax.dot_general(q, k, NT_DIMS, preferred_element_type=F32)
        m_new = jnp.maximum(m, jnp.max(s, axis=-1, keepdims=True))
        alpha = jnp.exp2(m - m_new)
        p = jnp.exp2(s - m_new)
        acc = alpha * acc + lax.dot_general(p.astype(BF16), vt, NT_DIMS, preferred_element_type=F32)
        return m_new, acc

    m0 = jnp.full((tq, 1), -jnp.inf, F32)
    acc0 = jnp.zeros((tq, MLA_SLOT), F32)
    _, acc = lax.fori_loop(0, nk, body, (m0, acc0))
    o_ref[...] = (acc / acc[:, MLA_V:MLA_V + 1]).astype(BF16)


def _flash_bounded_kernel(q_ref, k_ref, vt_ref, o_ref, acc_ref, *, tk, sub):
    nk = k_ref.shape[0] // tk
    acc_ref[...] = jnp.zeros_like(acc_ref)
    q = q_ref[...]

    def body(j, carry):
        for a in range(tk // sub):
            ks = pl.multiple_of(j * tk + a * sub, sub)
            st = lax.dot_general(k_ref[pl.ds(ks, sub), :], q, NT_DIMS, preferred_element_type=F32)
            acc_ref[...] += jnp.dot(vt_ref[:, pl.ds(ks, sub)], jnp.exp2(st).astype(BF16),
                                    preferred_element_type=F32)
        return carry

    lax.fori_loop(0, nk, body, 0, unroll=4)
    acc = acc_ref[...]
    o_ref[...] = jnp.transpose(acc / acc[MLA_V:MLA_V + 1, :]).astype(BF16)


def _flash_call(kernel_fn, q, k, vt, tq, scratch, name):
    B, S, wide = q.shape
    H = wide // MLA_SLOT
    qspec = pl.BlockSpec((None, tq, MLA_SLOT), lambda b, h, i: (b, i, h))
    kspec = pl.BlockSpec((None, S, MLA_SLOT), lambda b, h, i: (b, 0, h))
    vtspec = pl.BlockSpec((None, MLA_SLOT, S), lambda b, h, i: (b, h, 0))
    return pl.pallas_call(
        kernel_fn,
        out_shape=jax.ShapeDtypeStruct((B, S, wide), BF16),
        grid=(B, H, S // tq),
        in_specs=[qspec, kspec, vtspec],
        out_specs=qspec,
        scratch_shapes=scratch,
        compiler_params=_params("parallel", "parallel", "arbitrary"), name=name,
    )(q, k, vt)


def _flash(q, k, vt, bounded):
    fast = functools.partial(
        _flash_call, functools.partial(_flash_bounded_kernel, tk=1024, sub=512),
        tq=2048, scratch=[pltpu.VMEM((MLA_SLOT, 2048), F32)], name="mla_flash_bounded")
    general = functools.partial(
        _flash_call, functools.partial(_flash_kernel, tk=512),
        tq=512, scratch=[], name="mla_flash")
    return lax.cond(bounded, fast, general, q, k, vt)


def _mla_logit_bound(g_q, g_k):
    def seg(g):
        return MLA_NOPE * jnp.max(jnp.abs(g[:MLA_NOPE])) ** 2 + MLA_ROPE * jnp.max(jnp.abs(g[MLA_NOPE:])) ** 2
    return 1.05 * jnp.sqrt(seg(g_q) * seg(g_k)) * MLA_QK ** -0.5 * LOG2E


def _pack_halves(y):
    w = y.shape[1] // 2
    lo = pltpu.bitcast(y[:, :w].astype(BF16).astype(F32), jnp.int32)
    hi = pltpu.bitcast(y[:, w:].astype(BF16).astype(F32), jnp.int32)
    return (hi & HIGH_HALF) | lax.shift_right_logical(lo, 16)


def _unpack_halves(p):
    return pltpu.bitcast(p << 16, F32), pltpu.bitcast(p & HIGH_HALF, F32)


def _router_kernel(x_ref, mods_ref, g_ref, wrt_ref, br_ref, tri_ref,
                   hn_ref, eid_ref, wts_ref, rank_ref, cnt_ref, run_ref):
    @pl.when((pl.program_id(0) == 0) & (pl.program_id(1) == 0))
    def _():
        run_ref[...] = jnp.zeros_like(run_ref)

    hn = _modnorm(x_ref[...], g_ref[...], mods_ref[4:5, :], mods_ref[3:4, :])
    hn_ref[...] = _pack_halves(hn)
    tm = hn.shape[0]
    logits = lax.dot_general(wrt_ref[...], hn, NT_DIMS, preferred_element_type=F32,
                             precision=lax.Precision.HIGHEST)
    scores = _sigmoid(logits)
    choice = scores + br_ref[...]
    per = N_EXPERTS // N_GROUPS
    sub8 = lax.broadcasted_iota(jnp.int32, (per, tm), 0).astype(F32)
    gs = []
    for gi in range(N_GROUPS):
        cg = choice[gi * per:(gi + 1) * per, :]
        m1 = jnp.max(cg, axis=0, keepdims=True)
        i1 = jnp.min(jnp.where(cg == m1, sub8, float(per)), axis=0, keepdims=True)
        m2 = jnp.max(jnp.where(sub8 == i1, -jnp.inf, cg), axis=0, keepdims=True)
        gs.append(m1 + m2)
    gs8 = jnp.concatenate(gs, axis=0)
    rank = jnp.zeros_like(gs8)
    for gi in range(N_GROUPS):
        row = gs8[gi:gi + 1, :]
        ahead = (row > gs8) | ((row == gs8) & (sub8 > float(gi)))
        rank = rank + jnp.where(ahead, 1.0, 0.0)
    cur = jnp.concatenate(
        [jnp.where(rank[gi:gi + 1, :] < float(TOPK_GROUPS),
                   choice[gi * per:(gi + 1) * per, :], -jnp.inf)
         for gi in range(N_GROUPS)], axis=0)
    sub64 = lax.broadcasted_iota(jnp.int32, (N_EXPERTS, tm), 0).astype(F32)
    hits = []
    for _ in range(TOP_K):
        m = jnp.max(cur, axis=0, keepdims=True)
        idx = jnp.min(jnp.where(cur == m, sub64, float(N_EXPERTS)), axis=0, keepdims=True)
        hit = sub64 == idx
        hits.append((hit, idx))
        cur = jnp.where(hit, -jnp.inf, cur)
    chosen = jnp.zeros_like(cur)
    for hit, _ in hits:
        chosen = jnp.where(hit, 1.0, chosen)
    prefix = jnp.dot(chosen.astype(BF16), tri_ref[...], preferred_element_type=F32)
    rank_full = run_ref[:, 0:1] + prefix
    slot = lax.broadcasted_iota(jnp.int32, (8, tm), 0)
    eid8 = jnp.zeros((8, tm), F32)
    rank8 = jnp.zeros((8, tm), F32)
    w8 = jnp.zeros((8, tm), F32)
    for k, (hit, idx) in enumerate(hits):
        eid8 = jnp.where(slot == k, idx, eid8)
        rank8 = jnp.where(slot == k, jnp.sum(jnp.where(hit, rank_full, 0.0), axis=0, keepdims=True), rank8)
        w8 = jnp.where(slot == k, jnp.sum(jnp.where(hit, scores, 0.0), axis=0, keepdims=True), w8)
    eid_ref[...] = eid8.astype(jnp.int32)
    rank_ref[...] = rank8.astype(jnp.int32)
    wts_ref[...] = w8 / jnp.sum(w8, axis=0, keepdims=True) * ROUTED_SCALE
    run_ref[...] = run_ref[...] + jnp.sum(chosen, axis=1, keepdims=True)
    cnt_ref[...] = run_ref[...]


def _router(x, mods, g, wrt, br, tm=512):
    B, S, _ = x.shape
    tok = pl.BlockSpec((None, tm, D_MODEL), lambda b, i: (b, i, 0))
    slots = pl.BlockSpec((None, 8, tm), lambda b, i: (b, 0, i))
    tri = jnp.asarray(np.triu(np.ones((tm, tm), np.float32), 1), BF16)
    const = lambda a: pl.BlockSpec(a.shape, lambda b, i: (0, 0))
    return pl.pallas_call(
        _router_kernel,
        out_shape=(jax.ShapeDtypeStruct((B, S, D_MODEL // 2), jnp.int32),
                   jax.ShapeDtypeStruct((B, 8, S), jnp.int32),
                   jax.ShapeDtypeStruct((B, 8, S), F32),
                   jax.ShapeDtypeStruct((B, 8, S), jnp.int32),
                   jax.ShapeDtypeStruct((N_EXPERTS, 128), F32)),
        grid=(B, S // tm),
        in_specs=[tok, pl.BlockSpec((None, 8, D_MODEL), lambda b, i: (b, 0, 0)),
                  const(g), const(wrt), const(br), const(tri)],
        out_specs=(pl.BlockSpec((None, tm, D_MODEL // 2), lambda b, i: (b, i, 0)),
                   slots, slots, slots,
                   pl.BlockSpec((N_EXPERTS, 128), lambda b, i: (0, 0))),
        scratch_shapes=[pltpu.VMEM((N_EXPERTS, 128), F32)],
        compiler_params=_params("arbitrary", "arbitrary"), name="moe_router",
    )(x, mods, g, wrt, br, tri)


def _sc_gather(table, idx):
    R = idx.shape[0]
    W = table.shape[1]
    per_w = R // SC_WORKERS
    steps = per_w // SC_GATHER_ROWS
    assert per_w * SC_WORKERS == R and steps * SC_GATHER_ROWS == per_w and steps % 2 == 0
    idx3 = idx.reshape(SC_WORKERS, steps, SC_GATHER_ROWS)
    mesh = plsc.VectorSubcoreMesh(core_axis_name="c", subcore_axis_name="s")

    @functools.partial(
        pl.kernel, mesh=mesh,
        out_type=jax.ShapeDtypeStruct((R, W), jnp.int32),
        scratch_types=[pltpu.VMEM((steps, SC_GATHER_ROWS), jnp.int32),
                       pltpu.VMEM((2, SC_GATHER_ROWS, W), jnp.int32),
                       pltpu.SemaphoreType.DMA((2,)),
                       pltpu.SemaphoreType.DMA((2,))],
    )
    def gather_kernel(table_hbm, idx_hbm, out_hbm, idx_v, rows_v, gsem, wsem):
        wid = lax.axis_index("s") * SC_CORES + lax.axis_index("c")
        base = wid * per_w
        pltpu.sync_copy(idx_hbm.at[wid], idx_v)

        def gather(j, b):
            return pltpu.make_async_copy(table_hbm.at[idx_v.at[j]], rows_v.at[b], gsem.at[b])

        def write(j, b):
            return pltpu.make_async_copy(
                rows_v.at[b], out_hbm.at[pl.ds(base + j * SC_GATHER_ROWS, SC_GATHER_ROWS)], wsem.at[b])

        gather(0, 0).start()
        gather(1, 1).start()

        @pl.loop(0, steps, step=2)
        def _(j):
            for b in range(2):
                gather(j + b, b).wait()
                write(j + b, b).start()
                write(j + b, b).wait()

                @pl.when(j + b + 2 < steps)
                def _():
                    gather(j + b + 2, b).start()

    return gather_kernel(table, idx3)


def _sc_scatter(rows, idx, n_out):
    A = idx.shape[0]
    V, W = rows.shape
    per_w = A // SC_WORKERS
    steps = per_w // SC_GATHER_ROWS
    assert per_w * SC_WORKERS == A and steps * SC_GATHER_ROWS == per_w and steps % 2 == 0
    assert V % SC_GATHER_ROWS == 0
    idx3 = idx.reshape(SC_WORKERS, steps, SC_GATHER_ROWS)
    mesh = plsc.VectorSubcoreMesh(core_axis_name="c", subcore_axis_name="s")

    @functools.partial(
        pl.kernel, mesh=mesh,
        out_type=jax.ShapeDtypeStruct((n_out, W), jnp.int32),
        scratch_types=[pltpu.VMEM((steps, SC_GATHER_ROWS), jnp.int32),
                       pltpu.VMEM((2, SC_GATHER_ROWS, W), jnp.int32),
                       pltpu.SemaphoreType.DMA((2,)),
                       pltpu.SemaphoreType.DMA((2,))],
    )
    def scatter_kernel(rows_hbm, idx_hbm, out_hbm, idx_v, buf_v, rsem, wsem):
        wid = lax.axis_index("s") * SC_CORES + lax.axis_index("c")
        base = wid * per_w
        pltpu.sync_copy(idx_hbm.at[wid], idx_v)

        def read(j, b):
            src = lax.rem(base + j * SC_GATHER_ROWS, V)
            return pltpu.make_async_copy(rows_hbm.at[pl.ds(src, SC_GATHER_ROWS)], buf_v.at[b], rsem.at[b])

        def write(j, b):
            return pltpu.make_async_copy(buf_v.at[b], out_hbm.at[idx_v.at[j]], wsem.at[b])

        read(0, 0).start()
        read(1, 1).start()

        @pl.loop(0, steps, step=2)
        def _(j):
            for b in range(2):
                read(j + b, b).wait()
                write(j + b, b).start()
                write(j + b, b).wait()

                @pl.when(j + b + 2 < steps)
                def _():
                    read(j + b + 2, b).start()

    return scatter_kernel(rows, idx3)


def _dispatch_plan(eid, rank, counts, n_rows):
    eid = eid[:, :TOP_K, :]
    rank = rank[:, :TOP_K, :]
    counts = counts[:, 0].astype(jnp.int32)
    padded = (counts + EXPERT_TILE - 1) // EXPERT_TILE * EXPERT_TILE
    ends = jnp.cumsum(padded)
    offs = ends - padded
    experts = jnp.arange(N_EXPERTS, dtype=jnp.int32)
    row = rank + jnp.sum(jnp.where(eid[..., None] == experts, offs, 0), axis=-1)
    row_slotmajor = jnp.transpose(row, (1, 0, 2)).reshape(-1)
    tile_start = jnp.arange(n_rows // EXPERT_TILE, dtype=jnp.int32) * EXPERT_TILE
    tile_expert = jnp.minimum(jnp.sum(tile_start[:, None] >= ends[None, :], axis=1), N_EXPERTS - 1)
    live_end = jnp.sum(jnp.where(tile_expert[:, None] == experts, offs + counts, 0), axis=1)
    tile_live = jnp.clip(live_end - tile_start, 0, EXPERT_TILE)
    n_tiles = (ends[-1] // EXPERT_TILE).reshape(1)
    return (row_slotmajor, tile_expert.astype(jnp.int32), tile_live.astype(jnp.int32),
            n_tiles.astype(jnp.int32))


def _experts_kernel(te_ref, tl_ref, nt_ref, xs_ref, wg_ref, wu_ref, wd_ref, y_ref):
    i = pl.program_id(0)

    @pl.when(i < nt_ref[0])
    def _():
        live = lax.broadcasted_iota(jnp.int32, (EXPERT_TILE, 1), 0) < tl_ref[i]
        lo, hi = _unpack_halves(jnp.where(live, xs_ref[...], 0))
        x = jnp.concatenate([lo.astype(BF16), hi.astype(BF16)], axis=1)
        a = jnp.dot(x, wg_ref[...].astype(BF16), preferred_element_type=F32)
        u = jnp.dot(x, wu_ref[...].astype(BF16), preferred_element_type=F32)
        h = ((a * _sigmoid(a)) * u).astype(BF16)
        y_ref[...] = _pack_halves(jnp.dot(h, wd_ref[...].astype(BF16), preferred_element_type=F32))

    @pl.when(i >= nt_ref[0])
    def _():
        y_ref[...] = jnp.zeros_like(y_ref)


def _experts(xs, tile_expert, tile_live, n_tiles, wg, wu, wd, layer):
    n_rows, half = xs.shape
    rows = pl.BlockSpec((EXPERT_TILE, half), lambda i, te, tl, nt: (i, 0))
    w_in = pl.BlockSpec((None, None, D_MODEL, EXPERT_FF), lambda i, te, tl, nt: (layer, te[i], 0, 0))
    w_out = pl.BlockSpec((None, None, EXPERT_FF, D_MODEL), lambda i, te, tl, nt: (layer, te[i], 0, 0))
    return pl.pallas_call(
        _experts_kernel,
        out_shape=jax.ShapeDtypeStruct((n_rows, half), jnp.int32),
        grid_spec=pltpu.PrefetchScalarGridSpec(
            num_scalar_prefetch=3,
            grid=(n_rows // EXPERT_TILE,),
            in_specs=[rows, w_in, w_in, w_out],
            out_specs=rows),
        compiler_params=_params("arbitrary"), name="moe_experts",
    )(tile_expert, tile_live, n_tiles, xs, wg, wu, wd)


def _combine_kernel(yg_ref, wts_ref, hn_ref, wsg_ref, wsu_ref, wsd_ref, x_ref, mods_ref, out_ref):
    lo, hi = _unpack_halves(hn_ref[...])
    hn = jnp.concatenate([lo.astype(BF16), hi.astype(BF16)], axis=1)
    a = jnp.dot(hn, wsg_ref[...], preferred_element_type=F32)
    u = jnp.dot(hn, wsu_ref[...], preferred_element_type=F32)
    shared = jnp.dot(((a * _sigmoid(a)) * u).astype(BF16), wsd_ref[...], preferred_element_type=F32)
    half = D_MODEL // 2
    acc_lo = shared[:, :half]
    acc_hi = shared[:, half:]
    wts = wts_ref[...]
    for k in range(TOP_K):
        lo, hi = _unpack_halves(yg_ref[k])
        wk = wts[:, k:k + 1]
        acc_lo = acc_lo + wk * lo
        acc_hi = acc_hi + wk * hi
    g2 = mods_ref[5:6, :]
    out_ref[:, :half] = x_ref[:, :half] + g2[:, :half] * acc_lo
    out_ref[:, half:] = x_ref[:, half:] + g2[:, half:] * acc_hi


def _combine(yg, wts, hn, wsg, wsu, wsd, x, mods, tm=512):
    B, S, _ = x.shape
    half = D_MODEL // 2
    tok = pl.BlockSpec((None, tm, D_MODEL), lambda b, i: (b, i, 0))
    const = lambda a: pl.BlockSpec(a.shape, lambda b, i: (0,) * a.ndim)
    return pl.pallas_call(
        _combine_kernel,
        out_shape=jax.ShapeDtypeStruct((B, S, D_MODEL), F32),
        grid=(B, S // tm),
        in_specs=[pl.BlockSpec((TOP_K, None, tm, half), lambda b, i: (0, b, i, 0)),
                  pl.BlockSpec((None, tm, 8), lambda b, i: (b, i, 0)),
                  pl.BlockSpec((None, tm, half), lambda b, i: (b, i, 0)),
                  const(wsg), const(wsu), const(wsd),
                  tok, pl.BlockSpec((None, 8, D_MODEL), lambda b, i: (b, 0, 0))],
        out_specs=tok,
        compiler_params=_params("parallel", "parallel"), name="moe_combine",
    )(yg, wts, hn, wsg, wsu, wsd, x, mods)


def _block_diag_mean(segments, width):
    p = np.zeros((width, width), np.float32)
    pos = 0
    while pos < width:
        for length, live in segments:
            if live:
                p[pos:pos + length, pos:pos + length] = 1.0 / length
            pos += length
    return jnp.asarray(p, BF16)


def _slot_layout(w, n_heads, per_head, take, slot=MLA_SLOT):
    k = w.shape[0]
    wh = w.reshape(k, n_heads, per_head)[:, :, take]
    wh = jnp.pad(wh, ((0, 0), (0, 0), (0, slot - wh.shape[-1])))
    return wh.reshape(k, n_heads * slot)


def _slot_vector(pieces):
    v = jnp.concatenate(pieces)
    v = jnp.pad(v, (0, MLA_SLOT - v.shape[0]))
    return jnp.tile(v, 2).reshape(1, 2 * MLA_SLOT)


def _rope_tables(S):
    half = MLA_ROPE // 2
    inv = 1.0 / (ROPE_THETA ** (jnp.arange(half, dtype=F32) / half))
    ang = jnp.arange(S, dtype=F32)[:, None] * inv[None, :]
    cos, sin = jnp.cos(ang), jnp.sin(ang)
    ones = jnp.ones((S, MLA_NOPE), F32)
    pad = jnp.zeros((S, MLA_SLOT - MLA_QK), F32)
    cosf = jnp.concatenate([ones, cos, cos, pad], axis=1)
    sinf = jnp.concatenate([0.0 * ones, sin, sin, pad], axis=1)
    return jnp.tile(cosf, (1, 2)), jnp.tile(sinf, (1, 2))


def _prepare(g_norm1, g_norm2, na_w_qkv, na_g_q, na_g_k, na_rpb, na_w_o,
             mla_w_down, mla_g_cq, mla_g_ckv, mla_w_uq, mla_w_ukv, mla_g_q, mla_g_k, mla_w_o,
             w_router, b_router, w_gate, w_up, w_down, ws_gate, ws_up, ws_down):
    na = dict(
        g=g_norm1[0].reshape(1, D_MODEL),
        w=na_w_qkv[0].astype(BF16),
        p=_block_diag_mean([(NA_HEAD_DIM, True)], 256),
        gq=(jnp.tile(na_g_q[0], NA_HEADS) * (NA_HEAD_DIM ** -0.5 * LOG2E)).reshape(1, D_MODEL),
        gk=jnp.tile(na_g_k[0], NA_HEADS).reshape(1, D_MODEL),
        bias=_na_bias_table(na_rpb[0]),
        bounded=_na_logit_bound(na_g_q[0], na_g_k[0], na_rpb[0]) < FLASH_LOGIT_BOUND,
        wo=na_w_o[0].astype(BF16),
    )
    wdn = mla_w_down[0]
    kpe_cols = jnp.pad(wdn[:, MLA_Q_RANK + MLA_KV_RANK:],
                       ((0, 0), (MLA_NOPE, MLA_SLOT - MLA_QK)))
    nope = np.arange(MLA_NOPE)
    wuq = mla_w_uq[0]
    wukv = mla_w_ukv[0]
    wo = mla_w_o[0].reshape(MLA_HEADS, MLA_V, D_MODEL)
    wo = jnp.pad(wo, ((0, 0), (0, MLA_SLOT - MLA_V), (0, 0))).reshape(MLA_HEADS * MLA_SLOT, D_MODEL)
    gq = mla_g_q[0] * (MLA_QK ** -0.5 * LOG2E)
    gk = mla_g_k[0]
    zeros_nope = jnp.zeros((MLA_NOPE,), F32)
    mla = dict(
        g=g_norm1[1].reshape(1, D_MODEL),
        wd=jnp.concatenate([wdn[:, :MLA_Q_RANK + MLA_KV_RANK], kpe_cols], axis=1).astype(BF16),
        gcq=mla_g_cq[0].reshape(1, MLA_Q_RANK),
        gckv=mla_g_ckv[0].reshape(1, MLA_KV_RANK),
        wuq=_slot_layout(wuq, MLA_HEADS, MLA_QK, np.arange(MLA_QK)).astype(BF16),
        wukv=_slot_layout(wukv, MLA_HEADS, MLA_NOPE + MLA_V, nope).astype(BF16),
        wvt=_slot_layout(wukv, MLA_HEADS, MLA_NOPE + MLA_V, MLA_NOPE + np.arange(MLA_V)).T.astype(BF16),
        pq=_block_diag_mean([(MLA_NOPE, True), (MLA_ROPE, True), (MLA_SLOT - MLA_QK, False)], 256),
        pk=_block_diag_mean([(MLA_NOPE, True), (MLA_SLOT - MLA_NOPE, False)], 256),
        gq=_slot_vector([gq]),
        gkn=_slot_vector([gk[:MLA_NOPE]]),
        gkpe=_slot_vector([zeros_nope, gk[MLA_NOPE:]])[:, :MLA_SLOT],
        wo=wo.astype(BF16),
        bounded=_mla_logit_bound(mla_g_q[0], mla_g_k[0]) < FLASH_LOGIT_BOUND,
    )
    moe = []
    for i in range(w_router.shape[0]):
        moe.append(dict(
            g=g_norm2[i].reshape(1, D_MODEL),
            wrt=w_router[i].T,
            br=b_router[i].reshape(N_EXPERTS, 1),
            wg=w_gate, wu=w_up, wd=w_down, layer=i,
            wsg=ws_gate[i].astype(BF16), wsu=ws_up[i].astype(BF16), wsd=ws_down[i].astype(BF16),
        ))
    return na, mla, moe


def _moe_layer(x, mods, p):
    B, S, _ = x.shape
    half = D_MODEL // 2
    hn, eid, wts, rank, counts = _router(x, mods, p["g"], p["wrt"], p["br"])
    n_rows = B * S * TOP_K + N_EXPERTS * EXPERT_TILE
    rows, tile_expert, tile_live, n_tiles = _dispatch_plan(eid, rank, counts, n_rows)
    xs = _sc_scatter(hn.reshape(B * S, half), rows, n_rows)
    yield
    ys = _experts(xs, tile_expert, tile_live, n_tiles, p["wg"], p["wu"], p["wd"], p["layer"])
    yg = _sc_gather(ys, rows).reshape(TOP_K, B, S, half)
    yield
    return _combine(yg, jnp.swapaxes(wts, 1, 2), hn, p["wsg"], p["wsu"], p["wsd"], x, mods)


def _trunk(x, mods0, mods1, na, mla, moe):
    S = x.shape[1]
    q, k, v = _na_qkv(x, mods0, na["g"], na["w"], na["p"], na["gq"], na["gk"])
    o = _na_attn(q, k, v, na["bias"], na["bounded"])
    x = _proj_res(o, na["wo"], x, mods0, 2)
    x = yield from _moe_layer(x, mods0, moe[0])
    cosf, sinf = _rope_tables(S)
    q, k, vt = _mla_proj(x, mods1, mla["g"], mla["wd"], mla["gcq"], mla["gckv"], mla["wuq"],
                         mla["wukv"], mla["wvt"], mla["pq"], mla["pk"], mla["gq"], mla["gkn"],
                         mla["gkpe"], cosf, sinf)
    o = _flash(q, k, vt, mla["bounded"])
    x = _proj_res(o, mla["wo"], x, mods1, 2)
    x = yield from _moe_layer(x, mods1, moe[1])
    return x


def _interleave(generators):
    results = [None] * len(generators)
    live = list(range(len(generators)))
    while live:
        for i in list(live):
            try:
                next(generators[i])
            except StopIteration as done:
                results[i] = done.value
                live.remove(i)
    return results


def kernel(x_prompt, x_sample, c_prompt, c_sample, g_norm1, g_norm2, w_ada, b_ada, na_w_qkv, na_g_q, na_g_k, na_rpb, na_w_o, mla_w_down, mla_g_cq, mla_g_ckv, mla_w_uq, mla_w_ukv, mla_g_q, mla_g_k, mla_w_o, w_router, b_router, w_gate, w_up, w_down, ws_gate, ws_up, ws_down):
    na, mla, moe = _prepare(g_norm1, g_norm2, na_w_qkv, na_g_q, na_g_k, na_rpb, na_w_o,
                            mla_w_down, mla_g_cq, mla_g_ckv, mla_w_uq, mla_w_ukv, mla_g_q, mla_g_k,
                            mla_w_o, w_router, b_router, w_gate, w_up, w_down,
                            ws_gate, ws_up, ws_down)
    bp, bs = c_prompt.shape[0], c_sample.shape[0]
    c_all = jnp.concatenate([c_prompt, c_sample], axis=0)
    c_all = jnp.pad(c_all, ((0, (-c_all.shape[0]) % 8), (0, 0)))
    mods = []
    for i in range(w_ada.shape[0]):
        m = _mods(c_all, w_ada[i], b_ada[i]).reshape(c_all.shape[0], 6, D_MODEL)
        mods.append(jnp.pad(m, ((0, 0), (0, 2), (0, 0))))
    y_prompt, y_sample = _interleave([
        _trunk(x_prompt, mods[0][:bp], mods[1][:bp], na, mla, moe),
        _trunk(x_sample, mods[0][bp:bp + bs], mods[1][bp:bp + bs], na, mla, moe)])
    return (y_prompt, y_sample)
```
